```python
import math
import jax
import jax.numpy as jnp
from jax import lax
import numpy as np

D_MODEL = 1024
BATCH = 8
SEQ = 4096
DEPTH = 4

GRID_W = 64
CTX_LEN = 256

D_MIX = D_MODEL
SSD_HEADS = 12
SSD_HEAD_DIM = 64
SSD_WIDTH = SSD_HEADS * SSD_HEAD_DIM
SSD_GROUPS = 2
SSD_STATE = 128
SSD_CHUNK = 128
CONV_W = 3
XBC_WIDTH = SSD_WIDTH + 2 * SSD_GROUPS * SSD_STATE
SSD_IN = SSD_WIDTH + XBC_WIDTH + 2 * SSD_HEADS
S5_WIDTH = D_MIX - SSD_WIDTH
S5_GROUP = 16
S5_GROUPS = S5_WIDTH // S5_GROUP
S5_STATE = 64
IN_WIDTH = SSD_IN + S5_WIDTH

PEER_HEADS = 8
PEER_KEYS = 128
PEER_EXPERTS = PEER_KEYS * PEER_KEYS
PEER_QDIM = 256
PEER_HALF = PEER_QDIM // 2
PEER_TOPK = 16
PEER_BLOCK = 128

ALPHA = (2.0 * DEPTH) ** 0.25
BETA = (8.0 * DEPTH) ** -0.25
EPS = 1e-5

kernel_name = 'hymba_ssd_s5_peer_flow_block'


def layer_norm(x, g, b):
    xf = x.astype(jnp.float32)
    mu = jnp.mean(xf, -1, keepdims=True)
    var = jnp.mean(jnp.square(xf - mu), -1, keepdims=True)
    return ((xf - mu) * lax.rsqrt(var + EPS)).astype(x.dtype) * g + b


def rms_norm(x, g):
    xf = x.astype(jnp.float32)
    return (xf * lax.rsqrt(jnp.mean(xf * xf, -1, keepdims=True) + EPS)).astype(x.dtype) * g


def dwconv(x, w, b):
    y = lax.conv_general_dilated(x, w[:, None, :], window_strides=(1,),
                                 padding=[(CONV_W // 2, CONV_W // 2)],
                                 dimension_numbers=('NWC', 'WIO', 'NWC'),
                                 feature_group_count=x.shape[-1])
    return y + b


def ssd_scan(x, dt, A, Bm, Cm, h0):
    b, L, h, p = x.shape
    g, n = Bm.shape[2], Bm.shape[3]
    q = SSD_CHUNK
    nc = L // q
    hg = h // g
    x = x.reshape(b, nc, q, g, hg, p)
    dt = dt.reshape(b, nc, q, g, hg)
    Bm = Bm.reshape(b, nc, q, g, n)
    Cm = Cm.reshape(b, nc, q, g, n)
    a_cum = jnp.cumsum(dt * A.reshape(g, hg), axis=2)
    xdt = x * dt[..., None]
    seg = a_cum[:, :, :, None] - a_cum[:, :, None, :]
    mask = jnp.tril(jnp.ones((q, q), dtype=bool))[:, :, None, None]
    decay_ls = jnp.exp(jnp.where(mask, seg, -jnp.inf))
    cb = jnp.einsum('bclgn,bcsgn->bclsg', Cm, Bm)
    y_diag = jnp.einsum('bclsgh,bcsghp->bclghp', cb[..., None] * decay_ls, xdt)
    decay_s = jnp.exp(a_cum[:, :, -1:] - a_cum)
    states = jnp.einsum('bcsgn,bcsghp->bcghpn', Bm, xdt * decay_s[..., None])
    chunk_decay = jnp.exp(a_cum[:, :, -1])

    def step(hs, inp):
        dec, st = inp
        return hs * dec[..., None, None] + st, hs

    final, h_in = lax.scan(step, h0.reshape(b, g, hg, p, n),
                           (jnp.moveaxis(chunk_decay, 1, 0), jnp.moveaxis(states, 1, 0)))
    h_in = jnp.moveaxis(h_in, 0, 1)
    y_off = jnp.einsum('bclgn,bcghpn->bclghp', Cm, h_in) * jnp.exp(a_cum)[..., None]
    return (y_diag + y_off).reshape(b, L, h, p), final.reshape(b, h, p, n)


def ssd_branch(p_ctx, p_lat, conv_w, conv_b, a_log, dt_bias, d_skip, norm_g, ctx_out):
    A = -jnp.exp(a_log)

    def prep(pz):
        b, L, _ = pz.shape
        z = pz[..., :SSD_WIDTH]
        xbc = jax.nn.silu(dwconv(pz[..., SSD_WIDTH:SSD_WIDTH + XBC_WIDTH], conv_w, conv_b))
        xs = xbc[..., :SSD_WIDTH].reshape(b, L, SSD_HEADS, SSD_HEAD_DIM)
        bm = xbc[..., SSD_WIDTH:SSD_WIDTH + SSD_GROUPS * SSD_STATE].reshape(b, L, SSD_GROUPS, SSD_STATE)
        cm = xbc[..., SSD_WIDTH + SSD_GROUPS * SSD_STATE:].reshape(b, L, SSD_GROUPS, SSD_STATE)
        dt = jax.nn.softplus(pz[..., SSD_WIDTH + XBC_WIDTH:].reshape(b, L, 2, SSD_HEADS) + dt_bias)
        return z, xs, bm, cm, dt

    zc, xc, bc, cc, dtc = prep(p_ctx)
    zl, xl, bl, cl, dtl = prep(p_lat)
    h0 = jnp.zeros((xl.shape[0], SSD_HEADS, SSD_HEAD_DIM, SSD_STATE), xl.dtype)
    flip = lambda t: jnp.flip(t, 1)
    yc_f, hc_f = ssd_scan(xc, dtc[:, :, 0], A[0], bc, cc, h0)
    yl_f, _ = ssd_scan(xl, dtl[:, :, 0], A[0], bl, cl, hc_f)
    yc_b, hc_b = ssd_scan(flip(xc), flip(dtc[:, :, 1]), A[1], flip(bc), flip(cc), h0)
    yl_b, _ = ssd_scan(flip(xl), flip(dtl[:, :, 1]), A[1], flip(bl), flip(cl), hc_b)

    def finish(y, xs, z):
        y = y + xs * d_skip[:, None]
        y = y.reshape(y.shape[0], y.shape[1], SSD_WIDTH)
        return rms_norm(y * jax.nn.silu(z), norm_g)

    out_l = finish(yl_f + flip(yl_b), xl, zl)
    out_c = finish(yc_f + flip(yc_b), xc, zc) if ctx_out else None
    return out_l, out_c


def _cmul_scan(e1, e2):
    a1r, a1i, b1r, b1i = e1
    a2r, a2i, b2r, b2i = e2
    return (a2r * a1r - a2i * a1i,
            a2r * a1i + a2i * a1r,
            a2r * b1r - a2i * b1i + b2r,
            a2r * b1i + a2i * b1r + b2i)


def s5_branch(u_ctx, u_lat, lam_re, lam_im, log_step, b_re, b_im, c_re, c_im, d_skip,
              glu_w, glu_b, ctx_out):
    bsz, n_lat, _ = u_lat.shape
    rows = n_lat // GRID_W
    u_l = u_lat.reshape(bsz, rows, GRID_W, S5_WIDTH).transpose(0, 2, 1, 3)
    u_l = u_l.reshape(bsz, n_lat, S5_GROUPS, S5_GROUP)
    u_c = u_ctx.reshape(bsz, u_ctx.shape[1], S5_GROUPS, S5_GROUP)
    step = jnp.exp(log_step)[..., None]
    mag = jnp.exp(lam_re * step)
    ar = mag * jnp.cos(lam_im * step)
    ai = mag * jnp.sin(lam_im * step)
    den = lam_re * lam_re + lam_im * lam_im
    kr = ((ar - 1.0) * lam_re + ai * lam_im) / den
    ki = (ai * lam_re - (ar - 1.0) * lam_im) / den
    bbr = kr[..., None] * b_re - ki[..., None] * b_im
    bbi = kr[..., None] * b_im + ki[..., None] * b_re

    def scan_dir(u, d, h0r, h0i, reverse):
        bur = jnp.einsum('blgc,gpc->blgp', u, bbr[d])
        bui = jnp.einsum('blgc,gpc->blgp', u, bbi[d])
        first = -1 if reverse else 0
        bur = bur.at[:, first].add(ar[d] * h0r - ai[d] * h0i)
        bui = bui.at[:, first].add(ar[d] * h0i + ai[d] * h0r)
        a_r = jnp.broadcast_to(ar[d], bur.shape)
        a_i = jnp.broadcast_to(ai[d], bui.shape)
        _, _, sr, si = lax.associative_scan(_cmul_scan, (a_r, a_i, bur, bui),
                                            reverse=reverse, axis=1)
        return sr, si

    zero = jnp.zeros((bsz, S5_GROUPS, S5_STATE), u_lat.dtype)
    cfr, cfi = scan_dir(u_c, 0, zero, zero, False)
    cbr, cbi = scan_dir(u_c, 1, zero, zero, True)
    lfr, lfi = scan_dir(u_l, 0, cfr[:, -1], cfi[:, -1], False)
    lbr, lbi = scan_dir(u_l, 1, cbr[:, 0], cbi[:, 0], True)

    def readout(sr, si, u):
        y = (jnp.einsum('gcp,blgp->blgc', c_re, sr) - jnp.einsum('gcp,blgp->blgc', c_im, si)
             + u * d_skip.reshape(S5_GROUPS, S5_GROUP))
        y = jax.nn.gelu(y.reshape(y.shape[0], y.shape[1], S5_WIDTH))
        return y * jax.nn.sigmoid(y @ glu_w + glu_b)

    y_l = readout(lfr + lbr, lfi + lbi, u_l)
    y_l = y_l.reshape(bsz, GRID_W, rows, S5_WIDTH).transpose(0, 2, 1, 3).reshape(bsz, n_lat, S5_WIDTH)
    y_c = readout(cfr + cbr, cfi + cbi, u_c) if ctx_out else None
    return y_l, y_c


def peer_ffn(h, w_q, sub_keys, u_tab, v_tab):
    def block(hb):
        q = (hb @ w_q).reshape(hb.shape[0], PEER_HEADS, 2, PEER_HALF)
        s = jnp.einsum('thid,hikd->thik', q, sub_keys)
        sv, si = lax.top_k(s, PEER_TOPK)
        cand = (sv[:, :, 0, :, None] + sv[:, :, 1, None, :]).reshape(hb.shape[0], PEER_HEADS, PEER_TOPK * PEER_TOPK)
        cidx = (si[:, :, 0, :, None] * PEER_KEYS + si[:, :, 1, None, :]).reshape(hb.shape[0], PEER_HEADS, PEER_TOPK * PEER_TOPK)
        top_s, top_pos = lax.top_k(cand, PEER_TOPK)
        eidx = jnp.take_along_axis(cidx, top_pos, axis=-1)
        gate = jax.nn.softmax(top_s.astype(jnp.float32), axis=-1).astype(hb.dtype)
        act = jax.nn.gelu(jnp.einsum('thkd,td->thk', u_tab[eidx], hb))
        return jnp.einsum('thk,thkd->td', gate * act, v_tab[eidx])

    out = lax.map(block, h.reshape(-1, PEER_BLOCK, h.shape[-1]))
    return out.reshape(h.shape[0], h.shape[-1])


def setup_inputs(seed: int = 0) -> dict:
    key = jax.random.key(seed)
    ks = iter(jax.random.split(key, 40))
    nrm = lambda shape, s: jax.random.normal(next(ks), shape, jnp.float32) * s
    unif = lambda shape, lo, hi: jax.random.uniform(next(ks), shape, jnp.float32, lo, hi)
    nl = DEPTH
    x = nrm((BATCH, SEQ, D_MODEL), 1.0)
    c = nrm((BATCH, D_MODEL), 1.0)
    ctx = nrm((BATCH, CTX_LEN, D_MODEL), 1.0)
    c_ctx = nrm((D_MODEL,), 1.0)
    w_mod = nrm((nl, D_MODEL, 6 * D_MODEL), D_MODEL ** -0.5)
    b_mod = nrm((nl, 6 * D_MODEL), 0.02)
    w_in = nrm((nl, D_MODEL, IN_WIDTH), D_MODEL ** -0.5)
    conv_w = nrm((nl, CONV_W, XBC_WIDTH), CONV_W ** -0.5)
    conv_b = nrm((nl, XBC_WIDTH), 0.02)
    ssd_a_log = jnp.log(unif((nl, 2, SSD_HEADS), 1.0, 16.0))
    dt0 = jnp.exp(unif((nl, 2, SSD_HEADS), math.log(1e-3), math.log(1e-1)))
    ssd_dt_bias = dt0 + jnp.log(-jnp.expm1(-dt0))
    ssd_d = 1.0 + nrm((nl, SSD_HEADS), 0.1)
    ssd_norm_g = 1.0 + nrm((nl, SSD_WIDTH), 0.02)
    n_idx = jnp.arange(S5_STATE, dtype=jnp.float32)
    s5_lam_re = -0.5 + nrm((nl, 2, S5_GROUPS, S5_STATE), 0.01)
    s5_lam_im = math.pi * n_idx + nrm((nl, 2, S5_GROUPS, S5_STATE), 0.01)
    s5_log_step = unif((nl, 2, S5_GROUPS), math.log(1e-3), math.log(1e-1))
    s5_b_re = nrm((nl, S5_GROUPS, S5_STATE, S5_GROUP), S5_GROUP ** -0.5)
    s5_b_im = nrm((nl, S5_GROUPS, S5_STATE, S5_GROUP), S5_GROUP ** -0.5)
    s5_c_re = nrm((nl, S5_GROUPS, S5_GROUP, S5_STATE), S5_STATE ** -0.5)
    s5_c_im = nrm((nl, S5_GROUPS, S5_GROUP, S5_STATE), S5_STATE ** -0.5)
    s5_d = nrm((nl, S5_WIDTH), 1.0)
    s5_glu_w = nrm((nl, S5_WIDTH, S5_WIDTH), S5_WIDTH ** -0.5)
    s5_glu_b = nrm((nl, S5_WIDTH), 0.02)
    w_out = nrm((nl, D_MIX, D_MODEL), BETA * D_MIX ** -0.5)
    ln1_g = 1.0 + nrm((nl, D_MODEL), 0.02)
    ln1_b = nrm((nl, D_MODEL), 0.02)
    peer_wq = nrm((nl, D_MODEL, PEER_HEADS * PEER_QDIM), D_MODEL ** -0.5)
    peer_keys = nrm((nl, PEER_HEADS, 2, PEER_KEYS, PEER_HALF), PEER_HALF ** -0.5)
    peer_u = nrm((nl, PEER_EXPERTS, D_MODEL), D_MODEL ** -0.5)
    peer_v = nrm((nl, PEER_EXPERTS, D_MODEL), BETA)
    ln2_g = 1.0 + nrm((nl, D_MODEL), 0.02)
    ln2_b = nrm((nl, D_MODEL), 0.02)
    return {'x': x, 'c': c, 'ctx': ctx, 'c_ctx': c_ctx, 'w_mod': w_mod, 'b_mod': b_mod,
            'w_in': w_in, 'conv_w': conv_w, 'conv_b': conv_b, 'ssd_a_log': ssd_a_log,
            'ssd_dt_bias': ssd_dt_bias, 'ssd_d': ssd_d, 'ssd_norm_g': ssd_norm_g,
            's5_lam_re': s5_lam_re, 's5_lam_im': s5_lam_im, 's5_log_step': s5_log_step,
            's5_b_re': s5_b_re, 's5_b_im': s5_b_im, 's5_c_re': s5_c_re, 's5_c_im': s5_c_im,
            's5_d': s5_d, 's5_glu_w': s5_glu_w, 's5_glu_b': s5_glu_b, 'w_out': w_out,
            'ln1_g': ln1_g, 'ln1_b': ln1_b, 'peer_wq': peer_wq, 'peer_keys': peer_keys,
            'peer_u': peer_u, 'peer_v': peer_v, 'ln2_g': ln2_g, 'ln2_b': ln2_b}


def reference(x, c, ctx, c_ctx, w_mod, b_mod, w_in, conv_w, conv_b, ssd_a_log, ssd_dt_bias,
              ssd_d, ssd_norm_g, s5_lam_re, s5_lam_im, s5_log_step, s5_b_re, s5_b_im,
              s5_c_re, s5_c_im, s5_d, s5_glu_w, s5_glu_b, w_out, ln1_g, ln1_b, peer_wq,
              peer_keys, peer_u, peer_v, ln2_g, ln2_b):
    bsz, n_lat, d = x.shape
    n_ctx = ctx.shape[1]
    xl, xc = x, ctx
    for i in range(DEPTH):
        last = i == DEPTH - 1
        ml = [m[:, None, :] for m in jnp.split(jax.nn.silu(c) @ w_mod[i] + b_mod[i], 6, axis=-1)]
        mc = jnp.split(jax.nn.silu(c_ctx) @ w_mod[i] + b_mod[i], 6, axis=-1)
        pl = (xl * (1.0 + ml[1]) + ml[0]) @ w_in[i]
        pc = (xc * (1.0 + mc[1]) + mc[0]) @ w_in[i]
        ssd_l, ssd_c = ssd_branch(pc[..., :SSD_IN], pl[..., :SSD_IN], conv_w[i], conv_b[i],
                                  ssd_a_log[i], ssd_dt_bias[i], ssd_d[i], ssd_norm_g[i], not last)
        s5_l, s5_c = s5_branch(pc[..., SSD_IN:], pl[..., SSD_IN:], s5_lam_re[i], s5_lam_im[i],
                               s5_log_step[i], s5_b_re[i], s5_b_im[i], s5_c_re[i], s5_c_im[i],
                               s5_d[i], s5_glu_w[i], s5_glu_b[i], not last)
        mix_l = jnp.concatenate([ssd_l, s5_l], axis=-1) @ w_out[i]
        xl = layer_norm(ALPHA * xl + ml[2] * mix_l, ln1_g[i], ln1_b[i])
        if not last:
            mix_c = jnp.concatenate([ssd_c, s5_c], axis=-1) @ w_out[i]
            xc = layer_norm(ALPHA * xc + mc[2] * mix_c, ln1_g[i], ln1_b[i])
        hl = (xl * (1.0 + ml[4]) + ml[3]).reshape(bsz * n_lat, d)
        if last:
            f_l = peer_ffn(hl, peer_wq[i], peer_keys[i], peer_u[i], peer_v[i])
        else:
            hc = (xc * (1.0 + mc[4]) + mc[3]).reshape(bsz * n_ctx, d)
            f = peer_ffn(jnp.concatenate([hc, hl], axis=0), peer_wq[i], peer_keys[i], peer_u[i], peer_v[i])
            f_l = f[bsz * n_ctx:]
            xc = layer_norm(ALPHA * xc + mc[5] * f[:bsz * n_ctx].reshape(bsz, n_ctx, d), ln2_g[i], ln2_b[i])
        xl = layer_norm(ALPHA * xl + ml[5] * f_l.reshape(bsz, n_lat, d), ln2_g[i], ln2_b[i])
    return xl
```

```python
import functools
import math

import jax
import jax.numpy as jnp
from jax import lax
from jax.experimental import pallas as pl
from jax.experimental.pallas import tpu as pltpu

GRID_W = 64

SSD_HEADS = 12
SSD_HEAD_DIM = 64
SSD_GROUPS = 2
SSD_STATE = 128
SSD_WIDTH = SSD_HEADS * SSD_HEAD_DIM
XBC_WIDTH = SSD_WIDTH + 2 * SSD_GROUPS * SSD_STATE
HEAD_LANES = 128
S5_GROUP = 16
S5_STATE = 64
PEER_TOPK = 16
EPS = 1e-5

F32 = jnp.float32
BF16 = jnp.bfloat16
HIGHEST = lax.Precision.HIGHEST
NEG_INF = float("-inf")

VMEM_LIMIT = 56 * 1024 * 1024


def _cparams(sem):
    return pltpu.CompilerParams(dimension_semantics=sem, vmem_limit_bytes=VMEM_LIMIT)


def _silu(x):
    return x * jax.nn.sigmoid(x)


def _softplus(x):
    return jnp.maximum(x, 0.0) + jnp.log1p(jnp.exp(-jnp.abs(x)))


def _layer_norm(y, g, b):
    mu = jnp.mean(y, axis=-1, keepdims=True)
    d = y - mu
    var = jnp.mean(d * d, axis=-1, keepdims=True)
    return d * lax.rsqrt(var + EPS) * g + b


def _mod_row(tile, n_ctx_tiles, lat_tiles_per_batch, ctx_row):
    return jnp.where(tile < n_ctx_tiles, ctx_row,
                     (tile - n_ctx_tiles) // lat_tiles_per_batch)


def _mod_kernel(c_ref, w_ref, b_ref, o_ref):
    o_ref[...] = jnp.dot(_silu(c_ref[...]), w_ref[...], precision=HIGHEST,
                         preferred_element_type=F32) + b_ref[...]


def _mod_table(cc, w_mod, b_mod):
    depth, d, d6 = w_mod.shape
    rows = cc.shape[0]
    nj = d6 // d
    return pl.pallas_call(
        _mod_kernel,
        grid=(depth, nj),
        in_specs=[pl.BlockSpec((rows, d), lambda l, j: (0, 0)),
                  pl.BlockSpec((None, d, d), lambda l, j: (l, 0, j)),
                  pl.BlockSpec((None, 1, d), lambda l, j: (l, 0, j))],
        out_specs=pl.BlockSpec((None, rows, d), lambda l, j: (l, 0, j)),
        out_shape=jax.ShapeDtypeStruct((depth, rows, d6), F32),
        compiler_params=_cparams(("arbitrary", "arbitrary")),
        name="mod_table",
    )(cc, w_mod, b_mod.reshape(depth, 1, d6))


def _in_proj_kernel(x_ref, mod_ref, w_ref, z_ref, xbc_ref, dt_ref, u_ref, *,
                    d, n_ctx_tiles, lat_tiles, ctx_row):
    row = _mod_row(pl.program_id(0), n_ctx_tiles, lat_tiles, ctx_row)
    m0 = mod_ref[pl.ds(row, 1), 0:d]
    m1 = mod_ref[pl.ds(row, 1), d:2 * d]
    xm = (x_ref[...] * (1.0 + m1) + m0).astype(BF16)
    p = jnp.dot(xm, w_ref[...], preferred_element_type=F32)
    o = 0
    for ref in (z_ref, xbc_ref, dt_ref, u_ref):
        w = ref.shape[-1]
        ref[...] = p[:, o:o + w]
        o += w


def _in_proj(x, mod, w, *, tm, n_ctx_tiles, lat_tiles, ctx_row, s5_width):
    t, d = x.shape
    widths = (SSD_WIDTH, XBC_WIDTH, 2 * HEAD_LANES, s5_width)
    kern = functools.partial(_in_proj_kernel, d=d, n_ctx_tiles=n_ctx_tiles,
                             lat_tiles=lat_tiles, ctx_row=ctx_row)
    return pl.pallas_call(
        kern,
        grid=(t // tm,),
        in_specs=[pl.BlockSpec((tm, d), lambda i: (i, 0)),
                  pl.BlockSpec(mod.shape, lambda i: (0, 0)),
                  pl.BlockSpec(w.shape, lambda i: (0, 0))],
        out_specs=[pl.BlockSpec((tm, wd), lambda i: (i, 0)) for wd in widths],
        out_shape=[jax.ShapeDtypeStruct((t, wd), F32) for wd in widths],
        compiler_params=_cparams(("arbitrary",)),
        name="in_proj",
    )(x, mod, w)


def _ssd_chunk_index(s, *, nctx, nlat, reverse):
    if not reverse:
        return s
    return jnp.where(s < nctx, nctx - 1 - s, nctx + nlat - 1 - (s - nctx))


def _ssd_row_block(b, c, *, nb, nctx, nlat):
    return jnp.where(c < nctx, b * nctx + c, nb * nctx + b * nlat + (c - nctx))


def _ssd_kernel(*refs, q, nctx, nlat, reverse, final):
    if final:
        (xbc_ref, prev_ref, next_ref, dt_ref, z_ref, yprev_ref, convw_ref, convb_ref,
         a_ref, dtb_ref, dskip_ref, ng_ref, out_ref, h_ref) = refs
    else:
        (xbc_ref, prev_ref, next_ref, dt_ref, convw_ref, convb_ref,
         a_ref, dtb_ref, out_ref, h_ref) = refs
    s = pl.program_id(1)
    c = _ssd_chunk_index(s, nctx=nctx, nlat=nlat, reverse=reverse)

    @pl.when(s == 0)
    def _():
        h_ref[...] = jnp.zeros_like(h_ref)

    x = xbc_ref[...]
    rows = lax.broadcasted_iota(jnp.int32, x.shape, 0)
    has_prev = jnp.logical_and(c != 0, c != nctx).astype(F32)
    has_next = jnp.logical_and(c != nctx - 1, c != nctx + nlat - 1).astype(F32)
    xm1 = jnp.where(rows == 0, prev_ref[7:8, :] * has_prev, pltpu.roll(x, 1, 0))
    xp1 = jnp.where(rows == q - 1, next_ref[0:1, :] * has_next, pltpu.roll(x, q - 1, 0))
    conv = (convw_ref[0:1, :] * xm1 + convw_ref[1:2, :] * x + convw_ref[2:3, :] * xp1
            + convb_ref[...])
    act = _silu(conv)
    xs = act[:, :SSD_WIDTH]
    bm = act[:, SSD_WIDTH:SSD_WIDTH + SSD_GROUPS * SSD_STATE]
    cm = act[:, SSD_WIDTH + SSD_GROUPS * SSD_STATE:]

    dt = _softplus(dt_ref[...] + dtb_ref[...])
    a = dt * a_ref[...]
    li = lax.broadcasted_iota(jnp.int32, (q, q), 0)
    si = lax.broadcasted_iota(jnp.int32, (q, q), 1)
    mask = (si >= li) if reverse else (si <= li)
    cum = jnp.dot(mask.astype(F32), a, precision=HIGHEST, preferred_element_type=F32)
    cum_t = cum.T
    edge = 0 if reverse else q - 1

    hh = lax.broadcasted_iota(jnp.int32, (HEAD_LANES, SSD_WIDTH), 0)
    ll = lax.broadcasted_iota(jnp.int32, (HEAD_LANES, SSD_WIDTH), 1)
    expand = (lax.shift_right_logical(ll, int(math.log2(SSD_HEAD_DIM))) == hh).astype(F32)
    dt_e = jnp.dot(dt, expand, precision=HIGHEST, preferred_element_type=F32)
    cum_e = jnp.dot(cum, expand, precision=HIGHEST, preferred_element_type=F32)
    tot_e = cum_e[edge:edge + 1, :]
    xdt = xs * dt_e
    xdec = (xdt * jnp.exp(tot_e - cum_e)).astype(BF16)
    xdt_b = xdt.astype(BF16)
    off_scale = jnp.exp(cum_e)

    gw = SSD_WIDTH // SSD_GROUPS
    heads_per_group = SSD_HEADS // SSD_GROUPS
    lane = lax.broadcasted_iota(jnp.int32, (q, 2 * SSD_HEAD_DIM), 1)
    h_old = h_ref[...]
    y_groups = []
    st_groups = []
    for g in range(SSD_GROUPS):
        bg = bm[:, g * SSD_STATE:(g + 1) * SSD_STATE]
        cg = cm[:, g * SSD_STATE:(g + 1) * SSD_STATE].astype(BF16)
        cb = lax.dot_general(cg, bg.astype(BF16), (((1,), (1,)), ((), ())),
                             preferred_element_type=F32)
        pairs = []
        for pr in range(heads_per_group // 2):
            lo = g * gw + pr * 2 * SSD_HEAD_DIM
            xp = xdt_b[:, lo:lo + 2 * SSD_HEAD_DIM]
            ys = []
            for k in range(2):
                h = g * heads_per_group + pr * 2 + k
                seg = cum[:, h:h + 1] - cum_t[h:h + 1, :]
                lm = jnp.exp(jnp.where(mask, seg, NEG_INF))
                ys.append(jnp.dot((cb * lm).astype(BF16), xp, preferred_element_type=F32))
            pairs.append(jnp.where(lane < SSD_HEAD_DIM, ys[0], ys[1]))
        y_diag = jnp.concatenate(pairs, axis=1)
        hg = h_old[:, g * gw:(g + 1) * gw]
        y_off = jnp.dot(cg, hg.astype(BF16), preferred_element_type=F32)
        y_groups.append(y_diag + y_off * off_scale[:, g * gw:(g + 1) * gw])
        st_groups.append(jnp.dot(bg.T.astype(BF16), xdec[:, g * gw:(g + 1) * gw],
                                 preferred_element_type=F32))
    y = jnp.concatenate(y_groups, axis=1)
    h_ref[...] = h_old * jnp.exp(tot_e) + jnp.concatenate(st_groups, axis=1)

    if final:
        y = y + yprev_ref[...] + xs * dskip_ref[...]
        gated = y * _silu(z_ref[...])
        ms = jnp.mean(gated * gated, axis=-1, keepdims=True)
        out_ref[...] = gated * lax.rsqrt(ms + EPS) * ng_ref[...]
    else:
        out_ref[...] = y


def _ssd_pass(xbc, dt, z, yprev, convw, convb, a_row, dtb_row, dskip, ng, *,
              nb, ctx_len, seq, q, direction):
    t = xbc.shape[0]
    nctx, nlat = ctx_len // q, seq // q
    reverse = direction == 1
    final = direction == 1
    q8 = q // 8
    last8 = t // 8 - 1

    def blk(b, s):
        c = _ssd_chunk_index(s, nctx=nctx, nlat=nlat, reverse=reverse)
        return _ssd_row_block(b, c, nb=nb, nctx=nctx, nlat=nlat)

    row_map = lambda b, s: (blk(b, s), 0)
    prev_map = lambda b, s: (jnp.maximum(blk(b, s) * q8 - 1, 0), 0)
    next_map = lambda b, s: (jnp.minimum((blk(b, s) + 1) * q8, last8), 0)
    dt_map = lambda b, s: (blk(b, s), direction)
    const = lambda b, s: (0, 0)

    in_specs = [pl.BlockSpec((q, XBC_WIDTH), row_map),
                pl.BlockSpec((8, XBC_WIDTH), prev_map),
                pl.BlockSpec((8, XBC_WIDTH), next_map),
                pl.BlockSpec((q, HEAD_LANES), dt_map)]
    args = [xbc, xbc, xbc, dt]
    if final:
        in_specs += [pl.BlockSpec((q, SSD_WIDTH), row_map)] * 2
        args += [z, yprev]
    in_specs += [pl.BlockSpec(convw.shape, const), pl.BlockSpec(convb.shape, const),
                 pl.BlockSpec(a_row.shape, const), pl.BlockSpec(dtb_row.shape, const)]
    args += [convw, convb, a_row, dtb_row]
    if final:
        in_specs += [pl.BlockSpec(dskip.shape, const), pl.BlockSpec(ng.shape, const)]
        args += [dskip, ng]
    kern = functools.partial(_ssd_kernel, q=q, nctx=nctx, nlat=nlat,
                             reverse=reverse, final=final)
    return pl.pallas_call(
        kern,
        grid=(nb, nctx + nlat),
        in_specs=in_specs,
        out_specs=pl.BlockSpec((q, SSD_WIDTH), row_map),
        out_shape=jax.ShapeDtypeStruct((t, SSD_WIDTH), F32),
        scratch_shapes=[pltpu.VMEM((SSD_STATE, SSD_WIDTH), F32)],
        compiler_params=_cparams(("arbitrary", "arbitrary")),
        name="ssd_bwd" if final else "ssd_fwd",
    )(*args)


def _s5_block_index(i, *, nctx, nblk, reverse):
    if not reverse:
        return i
    return jnp.where(i < nctx, nctx - 1 - i, nblk - 1 - (i - nctx))


def _s5_kernel(*refs, steps, nb, reverse, final):
    if final:
        (u_ref, yprev_ref, bfull_ref, abar_ref, cfull_ref, dskip_ref, gluw_ref, glub_ref,
         out_ref, state_ref, x_ref) = refs
    else:
        u_ref, bfull_ref, abar_ref, cfull_ref, out_ref, state_ref, x_ref = refs

    @pl.when(pl.program_id(0) == 0)
    def _():
        state_ref[...] = jnp.zeros_like(state_ref)

    u = u_ref[...]
    x_ref[...] = jnp.dot(u.astype(BF16), bfull_ref[...], preferred_element_type=F32)
    n = abar_ref.shape[1] // 2
    half = n // 2
    for part in range(2):
        re = slice(part * half, (part + 1) * half)
        im = slice(n + part * half, n + (part + 1) * half)
        ar = abar_ref[:, re]
        ai = abar_ref[:, im]

        def step(k, carry, re=re, im=im, ar=ar, ai=ai):
            xr, xi = carry
            kk = steps - 1 - k if reverse else k
            r0 = pl.multiple_of(kk * nb, nb)
            nr = ar * xr - ai * xi + x_ref[pl.ds(r0, nb), re]
            ni = ar * xi + ai * xr + x_ref[pl.ds(r0, nb), im]
            x_ref[pl.ds(r0, nb), re] = nr
            x_ref[pl.ds(r0, nb), im] = ni
            return nr, ni

        xr, xi = lax.fori_loop(0, steps, step, (state_ref[:, re], state_ref[:, im]),
                               unroll=4)
        state_ref[:, re] = xr
        state_ref[:, im] = xi

    y = jnp.dot(x_ref[...].astype(BF16), cfull_ref[...], preferred_element_type=F32)
    if final:
        y = jax.nn.gelu(y + yprev_ref[...] + u * dskip_ref[...])
        gate = jnp.dot(y.astype(BF16), gluw_ref[...], preferred_element_type=F32)
        out_ref[...] = y * jax.nn.sigmoid(gate + glub_ref[...])
    else:
        out_ref[...] = y


def _s5_pass(u_scan, yprev, bfull, abar, cfull, dskip, gluw, glub, *,
             nb, ctx_len, steps, direction):
    rows, width = u_scan.shape
    nblk = rows // (steps * nb)
    nctx = ctx_len // steps
    reverse = direction == 1
    final = direction == 1
    row_map = lambda i: (_s5_block_index(i, nctx=nctx, nblk=nblk, reverse=reverse), 0)
    const = lambda i: (0, 0)
    blk = pl.BlockSpec((steps * nb, width), row_map)
    in_specs = [blk]
    args = [u_scan]
    if final:
        in_specs.append(blk)
        args.append(yprev)
    in_specs += [pl.BlockSpec(bfull.shape, const), pl.BlockSpec(abar.shape, const),
                 pl.BlockSpec(cfull.shape, const)]
    args += [bfull, abar, cfull]
    if final:
        in_specs += [pl.BlockSpec(dskip.shape, const), pl.BlockSpec(gluw.shape, const),
                     pl.BlockSpec(glub.shape, const)]
        args += [dskip, gluw, glub]
    kern = functools.partial(_s5_kernel, steps=steps, nb=nb, reverse=reverse, final=final)
    return pl.pallas_call(
        kern,
        grid=(nblk,),
        in_specs=in_specs,
        out_specs=blk,
        out_shape=jax.ShapeDtypeStruct((rows, width), F32),
        scratch_shapes=[pltpu.VMEM(abar.shape, F32),
                        pltpu.VMEM((steps * nb, abar.shape[1]), F32)],
        compiler_params=_cparams(("arbitrary",)),
        name="s5_bwd" if final else "s5_fwd",
    )(*args)


def _s5_params(lam_re, lam_im, log_step, b_re, b_im, c_re, c_im, nb):
    g, p = lam_re.shape[1], lam_re.shape[2]
    step = jnp.exp(log_step)[..., None]
    mag = jnp.exp(lam_re * step)
    ar = mag * jnp.cos(lam_im * step)
    ai = mag * jnp.sin(lam_im * step)
    den = lam_re * lam_re + lam_im * lam_im
    kr = ((ar - 1.0) * lam_re + ai * lam_im) / den
    ki = (ai * lam_re - (ar - 1.0) * lam_im) / den
    bbr = kr[..., None] * b_re - ki[..., None] * b_im
    bbi = kr[..., None] * b_im + ki[..., None] * b_re
    eye = jnp.eye(g, dtype=F32)
    cdim = b_re.shape[-1]

    def bmat(bb):
        return jnp.einsum('gpc,gh->gchp', bb, eye).reshape(g * cdim, g * p)

    bfull = jnp.stack([jnp.concatenate([bmat(bbr[d]), bmat(bbi[d])], axis=1)
                       for d in range(2)]).astype(BF16)
    abar = jnp.stack([jnp.broadcast_to(
        jnp.concatenate([ar[d].reshape(-1), ai[d].reshape(-1)])[None, :], (nb, 2 * g * p))
        for d in range(2)])

    def cmat(cc):
        return jnp.einsum('gcp,gh->gphc', cc, eye).reshape(g * p, g * cdim)

    cfull = jnp.concatenate([cmat(c_re), -cmat(c_im)], axis=0).astype(BF16)
    return bfull, abar, cfull


def _out_proj_kernel(x_ref, ssd_ref, s5_ref, mod_ref, w1_ref, w2_ref, g_ref, b_ref, o_ref, *,
                     d, alpha, n_ctx_tiles, lat_tiles, ctx_row):
    row = _mod_row(pl.program_id(0), n_ctx_tiles, lat_tiles, ctx_row)
    m2 = mod_ref[pl.ds(row, 1), 2 * d:3 * d]
    mix = (jnp.dot(ssd_ref[...].astype(BF16), w1_ref[...], preferred_element_type=F32)
           + jnp.dot(s5_ref[...].astype(BF16), w2_ref[...], preferred_element_type=F32))
    o_ref[...] = _layer_norm(alpha * x_ref[...] + m2 * mix, g_ref[...], b_ref[...])


def _out_proj(x, ssd, s5, mod, w1, w2, g, b, *, tm, alpha, n_ctx_tiles, lat_tiles, ctx_row):
    t, d = x.shape
    kern = functools.partial(_out_proj_kernel, d=d, alpha=alpha, n_ctx_tiles=n_ctx_tiles,
                             lat_tiles=lat_tiles, ctx_row=ctx_row)
    const = lambda i: (0, 0)
    return pl.pallas_call(
        kern,
        grid=(t // tm,),
        in_specs=[pl.BlockSpec((tm, d), lambda i: (i, 0)),
                  pl.BlockSpec((tm, ssd.shape[1]), lambda i: (i, 0)),
                  pl.BlockSpec((tm, s5.shape[1]), lambda i: (i, 0)),
                  pl.BlockSpec(mod.shape, const), pl.BlockSpec(w1.shape, const),
                  pl.BlockSpec(w2.shape, const), pl.BlockSpec(g.shape, const),
                  pl.BlockSpec(b.shape, const)],
        out_specs=pl.BlockSpec((tm, d), lambda i: (i, 0)),
        out_shape=jax.ShapeDtypeStruct((t, d), F32),
        compiler_params=_cparams(("arbitrary",)),
        name="out_proj_ln",
    )(x, ssd, s5, mod, w1, w2, g, b)


def _peer_kernel(x_ref, mod_ref, wqt_ref, keys_ref, u_ref, vt_ref, g_ref, b_ref, o_ref,
                 hbt_ref, acc_ref, q_ref, top_ref, cand_ref, ord_ref, work_ref, s1_ref, e1_ref, th_ref,
                 r_ref, gate_ref, *, d, nk, heads, half, alpha, n_ctx_tiles, lat_tiles, ctx_row):
    i = pl.program_id(0)
    j = pl.program_id(1)
    tm = x_ref.shape[0]
    eb = u_ref.shape[0]
    nsel = PEER_TOPK + 1
    pad_rows = top_ref.shape[1]

    @pl.when(j == 0)
    def _prepare():
        row = _mod_row(i, n_ctx_tiles, lat_tiles, ctx_row)
        m3 = mod_ref[pl.ds(row, 1), 3 * d:4 * d]
        m4 = mod_ref[pl.ds(row, 1), 4 * d:5 * d]
        h = x_ref[...] * (1.0 + m4) + m3
        hbt = h.T.astype(BF16)
        hbt_ref[...] = hbt
        q_ref[...] = jnp.dot(wqt_ref[...], hbt, preferred_element_type=F32)
        acc_ref[...] = jnp.zeros_like(acc_ref)

        def per_head(hd, carry):
            top_ref[...] = jnp.full(top_ref.shape, NEG_INF, F32)
            for c, dst in ((0, th_ref), (1, s1_ref)):
                off = pl.multiple_of((hd * 2 + c) * nk, nk)
                qoff = pl.multiple_of((hd * 2 + c) * half, half)
                sc = jnp.dot(keys_ref[pl.ds(off, nk), :], q_ref[pl.ds(qoff, half), :],
                             precision=HIGHEST, preferred_element_type=F32)
                dst[hd] = sc
                work_ref[...] = sc

                def take(r, carry, c=c):
                    cur = work_ref[...]
                    m = jnp.max(cur, axis=0, keepdims=True)
                    top_ref[c, pl.ds(r, 1), :] = m
                    work_ref[...] = jnp.where(cur >= m, NEG_INF, cur)
                    return carry

                lax.fori_loop(0, nsel, take, 0)
            v0 = top_ref[0]
            v1 = top_ref[1]
            cands = [v0[0:1, :] + v1]
            cands += [v0[a:a + 1, :] + v1[0:8, :] for a in range(1, 8)]
            cands += [v0[8:pad_rows, :] + v1[0:1, :]]
            cand = jnp.concatenate(cands, axis=0)
            cand_ref[...] = cand

            def take2(r, carry):
                cur = cand_ref[...]
                m = jnp.max(cur, axis=0, keepdims=True)
                ord_ref[pl.ds(r, 1), :] = m
                cand_ref[...] = jnp.where(cur >= m, NEG_INF, cur)
                return carry

            lax.fori_loop(0, nsel, take2, 0)
            tau = 0.5 * (ord_ref[nsel - 2:nsel - 1, :] + ord_ref[nsel - 1:nsel, :])
            top = v0[0:1, :] + v1[0:1, :]
            z = jnp.sum(jnp.where(cand > tau, jnp.exp(cand - top), 0.0), axis=0, keepdims=True)
            s0 = th_ref[hd]
            s1 = s1_ref[hd]
            e1_ref[hd] = jnp.exp(s1 - v1[0:1, :])
            th_ref[hd] = tau - s0
            r_ref[hd] = jnp.exp(s0 - v0[0:1, :]) / z
            return carry

        lax.fori_loop(0, heads, per_head, 0)

    a_t = jnp.dot(u_ref[...], hbt_ref[...], preferred_element_type=F32)
    lanes = 128
    keys_per_step = eb // nk
    base = pl.multiple_of(j * keys_per_step, keys_per_step)
    for ii in range(keys_per_step):
        for lt in range(tm // lanes):
            ls = slice(lt * lanes, (lt + 1) * lanes)
            w = jnp.zeros((nk, lanes), F32)
            for hd in range(heads):
                th = th_ref[hd, pl.ds(base, keys_per_step), ls][ii:ii + 1, :]
                rr = r_ref[hd, pl.ds(base, keys_per_step), ls][ii:ii + 1, :]
                w = w + jnp.where(s1_ref[hd, :, ls] >= th, e1_ref[hd, :, ls], 0.0) * rr
            a = a_t[ii * nk:(ii + 1) * nk, ls]
            gate_ref[ii * nk:(ii + 1) * nk, ls] = (w * jax.nn.gelu(a)).astype(BF16)
    acc_ref[...] += jnp.dot(vt_ref[...], gate_ref[...], preferred_element_type=F32)

    @pl.when(j == pl.num_programs(1) - 1)
    def _finish():
        row = _mod_row(i, n_ctx_tiles, lat_tiles, ctx_row)
        m5 = mod_ref[pl.ds(row, 1), 5 * d:6 * d]
        f = acc_ref[...].T
        o_ref[...] = _layer_norm(alpha * x_ref[...] + m5 * f, g_ref[...], b_ref[...])


def _peer(x, mod, wqt, keys2d, u_b, vt_b, g, b, *, tm, eb, heads, alpha,
          n_ctx_tiles, lat_tiles, ctx_row):
    t, d = x.shape
    nexp = u_b.shape[0]
    nk = keys2d.shape[0] // (2 * heads)
    half = keys2d.shape[1]
    pad_rows = 24
    ncand = pad_rows + 7 * 8 + (pad_rows - 8)
    kern = functools.partial(_peer_kernel, d=d, nk=nk, heads=heads, half=half, alpha=alpha,
                             n_ctx_tiles=n_ctx_tiles, lat_tiles=lat_tiles, ctx_row=ctx_row)
    const = lambda i, j: (0, 0)
    return pl.pallas_call(
        kern,
        grid=(t // tm, nexp // eb),
        in_specs=[pl.BlockSpec((tm, d), lambda i, j: (i, 0)),
                  pl.BlockSpec(mod.shape, const), pl.BlockSpec(wqt.shape, const),
                  pl.BlockSpec(keys2d.shape, const),
                  pl.BlockSpec((eb, d), lambda i, j: (j, 0)),
                  pl.BlockSpec((d, eb), lambda i, j: (0, j)),
                  pl.BlockSpec(g.shape, const), pl.BlockSpec(b.shape, const)],
        out_specs=pl.BlockSpec((tm, d), lambda i, j: (i, 0)),
        out_shape=jax.ShapeDtypeStruct((t, d), F32),
        scratch_shapes=[pltpu.VMEM((d, tm), BF16),
                        pltpu.VMEM((d, tm), F32),
                        pltpu.VMEM((wqt.shape[0], tm), F32),
                        pltpu.VMEM((2, pad_rows, tm), F32),
                        pltpu.VMEM((ncand, tm), F32),
                        pltpu.VMEM((pad_rows, tm), F32),
                        pltpu.VMEM((nk, tm), F32),
                        pltpu.VMEM((heads, nk, tm), F32),
                        pltpu.VMEM((heads, nk, tm), F32),
                        pltpu.VMEM((heads, nk, tm), F32),
                        pltpu.VMEM((heads, nk, tm), F32),
                        pltpu.VMEM((eb, tm), BF16)],
        compiler_params=_cparams(("arbitrary", "arbitrary")),
        name="peer_ln",
    )(x, mod, wqt, keys2d, u_b, vt_b, g, b)


def _pick_tile(limit, *sizes):
    tm = limit
    while any(s % tm for s in sizes):
        tm //= 2
    return tm


def kernel(x, c, ctx, c_ctx, w_mod, b_mod, w_in, conv_w, conv_b, ssd_a_log, ssd_dt_bias, ssd_d, ssd_norm_g, s5_lam_re, s5_lam_im, s5_log_step, s5_b_re, s5_b_im, s5_c_re, s5_c_im, s5_d, s5_glu_w, s5_glu_b, w_out, ln1_g, ln1_b, peer_wq, peer_keys, peer_u, peer_v, ln2_g, ln2_b):
    nb, seq, d = x.shape
    ctx_len = ctx.shape[1]
    depth = w_mod.shape[0]
    s5_width = s5_d.shape[1]
    heads = peer_keys.shape[1]
    nk = peer_keys.shape[3]
    alpha = (2.0 * depth) ** 0.25
    rows_lat = seq // GRID_W

    tm = _pick_tile(256, nb * ctx_len, seq)
    tp = _pick_tile(512, nb * ctx_len, seq)
    q = _pick_tile(128, ctx_len, seq)
    steps = _pick_tile(64, ctx_len, seq)
    eb = 8 * nk
    ctx_row = nb

    mod_rows = -(-(nb + 1) // 8) * 8
    cc = jnp.concatenate([c, c_ctx[None, :], jnp.zeros((mod_rows - nb - 1, d), F32)], axis=0)
    mods = _mod_table(cc, w_mod, b_mod)

    xt = jnp.concatenate([ctx.reshape(nb * ctx_len, d), x.reshape(nb * seq, d)], axis=0)

    def pad_lanes(w, n):
        return jnp.pad(w, ((0, 0), (0, n - w.shape[1])))

    for i in range(depth):
        wi = w_in[i]
        o_dt = SSD_WIDTH + XBC_WIDTH
        w_pad = jnp.concatenate(
            [wi[:, :o_dt],
             pad_lanes(wi[:, o_dt:o_dt + SSD_HEADS], HEAD_LANES),
             pad_lanes(wi[:, o_dt + SSD_HEADS:o_dt + 2 * SSD_HEADS], HEAD_LANES),
             wi[:, o_dt + 2 * SSD_HEADS:]], axis=1).astype(BF16)
        a_rows = pad_lanes(-jnp.exp(ssd_a_log[i]), HEAD_LANES)
        dtb_rows = pad_lanes(ssd_dt_bias[i], HEAD_LANES)
        dskip_ssd = jnp.repeat(ssd_d[i], SSD_HEAD_DIM)[None, :]
        bfull, abar, cfull = _s5_params(s5_lam_re[i], s5_lam_im[i], s5_log_step[i],
                                        s5_b_re[i], s5_b_im[i], s5_c_re[i], s5_c_im[i], nb)

        z, xbc, dt, u = _in_proj(xt, mods[i], w_pad, tm=tm, n_ctx_tiles=nb * ctx_len // tm,
                                 lat_tiles=seq // tm, ctx_row=ctx_row, s5_width=s5_width)
        ssd_args = dict(nb=nb, ctx_len=ctx_len, seq=seq, q=q)
        y_f = _ssd_pass(xbc, dt, None, None, conv_w[i], conv_b[i][None, :], a_rows[0:1],
                        dtb_rows[0:1], None, None, direction=0, **ssd_args)
        ssd_out = _ssd_pass(xbc, dt, z, y_f, conv_w[i], conv_b[i][None, :], a_rows[1:2],
                            dtb_rows[1:2], dskip_ssd, ssd_norm_g[i][None, :], direction=1,
                            **ssd_args)

        u_ctx = u[:nb * ctx_len].reshape(nb, ctx_len, s5_width).transpose(1, 0, 2)
        u_lat = u[nb * ctx_len:].reshape(nb, rows_lat, GRID_W, s5_width).transpose(2, 1, 0, 3)
        u_scan = jnp.concatenate([u_ctx.reshape(ctx_len * nb, s5_width),
                                  u_lat.reshape(seq * nb, s5_width)], axis=0)
        s5_args = dict(nb=nb, ctx_len=ctx_len, steps=steps)
        ys_f = _s5_pass(u_scan, None, bfull[0], abar[0], cfull, None, None, None,
                        direction=0, **s5_args)
        s5_scan = _s5_pass(u_scan, ys_f, bfull[1], abar[1], cfull, s5_d[i][None, :],
                           s5_glu_w[i].astype(BF16), s5_glu_b[i][None, :], direction=1,
                           **s5_args)
        s5_ctx = s5_scan[:ctx_len * nb].reshape(ctx_len, nb, s5_width).transpose(1, 0, 2)
        s5_lat = s5_scan[ctx_len * nb:].reshape(GRID_W, rows_lat, nb, s5_width)
        s5_lat = s5_lat.transpose(2, 1, 0, 3)
        s5_out = jnp.concatenate([s5_ctx.reshape(nb * ctx_len, s5_width),
                                  s5_lat.reshape(nb * seq, s5_width)], axis=0)

        wo = w_out[i].astype(BF16)
        xt = _out_proj(xt, ssd_out, s5_out, mods[i], wo[:SSD_WIDTH], wo[SSD_WIDTH:],
                       ln1_g[i][None, :], ln1_b[i][None, :], tm=tm, alpha=alpha,
                       n_ctx_tiles=nb * ctx_len // tm, lat_tiles=seq // tm, ctx_row=ctx_row)

        xt = _peer(xt, mods[i], peer_wq[i].T.astype(BF16),
                   peer_keys[i].reshape(heads * 2 * nk, -1),
                   peer_u[i].astype(BF16), peer_v[i].T.astype(BF16),
                   ln2_g[i][None, :], ln2_b[i][None, :], tm=tp, eb=eb, heads=heads,
                   alpha=alpha, n_ctx_tiles=nb * ctx_len // tp, lat_tiles=seq // tp,
                   ctx_row=ctx_row)

    return xt[nb * ctx_len:].reshape(nb, seq, d)
```

```python
import functools
import math

import jax
import jax.numpy as jnp
from jax import lax
from jax.experimental import pallas as pl
from jax.experimental.pallas import tpu as pltpu

GRID_W = 64

SSD_HEADS = 12
SSD_HEAD_DIM = 64
SSD_GROUPS = 2
SSD_STATE = 128
SSD_WIDTH = SSD_HEADS * SSD_HEAD_DIM
XBC_WIDTH = SSD_WIDTH + 2 * SSD_GROUPS * SSD_STATE
HEAD_LANES = 128
S5_GROUP = 16
S5_STATE = 64
PEER_TOPK = 16
EPS = 1e-5

F32 = jnp.float32
BF16 = jnp.bfloat16
HIGHEST = lax.Precision.HIGHEST
NEG_INF = float("-inf")

VMEM_LIMIT = 56 * 1024 * 1024


def _cparams(sem):
    return pltpu.CompilerParams(dimension_semantics=sem, vmem_limit_bytes=VMEM_LIMIT)


def _silu(x):
    return x * jax.nn.sigmoid(x)


def _softplus(x):
    return jnp.maximum(x, 0.0) + jnp.log1p(jnp.exp(-jnp.abs(x)))


def _layer_norm(y, g, b):
    mu = jnp.mean(y, axis=-1, keepdims=True)
    d = y - mu
    var = jnp.mean(d * d, axis=-1, keepdims=True)
    return d * lax.rsqrt(var + EPS) * g + b


def _mod_row(tile, n_ctx_tiles, lat_tiles_per_batch, ctx_row):
    return jnp.where(tile < n_ctx_tiles, ctx_row,
                     (tile - n_ctx_tiles) // lat_tiles_per_batch)


def _mod_kernel(c_ref, w_ref, b_ref, o_ref):
    o_ref[...] = jnp.dot(_silu(c_ref[...]), w_ref[...], precision=HIGHEST,
                         preferred_element_type=F32) + b_ref[...]


def _mod_table(cc, w_mod, b_mod):
    depth, d, d6 = w_mod.shape
    rows = cc.shape[0]
    nj = d6 // d
    return pl.pallas_call(
        _mod_kernel,
        grid=(depth, nj),
        in_specs=[pl.BlockSpec((rows, d), lambda l, j: (0, 0)),
                  pl.BlockSpec((None, d, d), lambda l, j: (l, 0, j)),
                  pl.BlockSpec((None, 1, d), lambda l, j: (l, 0, j))],
        out_specs=pl.BlockSpec((None, rows, d), lambda l, j: (l, 0, j)),
        out_shape=jax.ShapeDtypeStruct((depth, rows, d6), F32),
        compiler_params=_cparams(("arbitrary", "arbitrary")),
        name="mod_table",
    )(cc, w_mod, b_mod.reshape(depth, 1, d6))


def _in_proj_kernel(x_ref, mod_ref, w_ref, z_ref, xbc_ref, dt_ref, u_ref, *,
                    d, n_ctx_tiles, lat_tiles, ctx_row):
    row = _mod_row(pl.program_id(0), n_ctx_tiles, lat_tiles, ctx_row)
    m0 = mod_ref[pl.ds(row, 1), 0:d]
    m1 = mod_ref[pl.ds(row, 1), d:2 * d]
    xm = (x_ref[...] * (1.0 + m1) + m0).astype(BF16)
    p = jnp.dot(xm, w_ref[...], preferred_element_type=F32)
    o = 0
    for ref in (z_ref, xbc_ref, dt_ref, u_ref):
        w = ref.shape[-1]
        ref[...] = p[:, o:o + w]
        o += w


def _in_proj(x, mod, w, *, tm, n_ctx_tiles, lat_tiles, ctx_row, s5_width):
    t, d = x.shape
    widths = (SSD_WIDTH, XBC_WIDTH, 2 * HEAD_LANES, s5_width)
    kern = functools.partial(_in_proj_kernel, d=d, n_ctx_tiles=n_ctx_tiles,
                             lat_tiles=lat_tiles, ctx_row=ctx_row)
    return pl.pallas_call(
        kern,
        grid=(t // tm,),
        in_specs=[pl.BlockSpec((tm, d), lambda i: (i, 0)),
                  pl.BlockSpec(mod.shape, lambda i: (0, 0)),
                  pl.BlockSpec(w.shape, lambda i: (0, 0))],
        out_specs=[pl.BlockSpec((tm, wd), lambda i: (i, 0)) for wd in widths],
        out_shape=[jax.ShapeDtypeStruct((t, wd), F32) for wd in widths],
        compiler_params=_cparams(("arbitrary",)),
        name="in_proj",
    )(x, mod, w)


def _ssd_chunk_index(s, *, nctx, nlat, reverse):
    if not reverse:
        return s
    return jnp.where(s < nctx, nctx - 1 - s, nctx + nlat - 1 - (s - nctx))


def _ssd_row_block(b, c, *, nb, nctx, nlat):
    return jnp.where(c < nctx, b * nctx + c, nb * nctx + b * nlat + (c - nctx))


def _ssd_kernel(*refs, q, nctx, nlat, reverse, final):
    if final:
        (xbc_ref, prev_ref, next_ref, dt_ref, z_ref, yprev_ref, convw_ref, convb_ref,
         a_ref, dtb_ref, dskip_ref, ng_ref, out_ref, h_ref) = refs
    else:
        (xbc_ref, prev_ref, next_ref, dt_ref, convw_ref, convb_ref,
         a_ref, dtb_ref, out_ref, h_ref) = refs
    s = pl.program_id(1)
    c = _ssd_chunk_index(s, nctx=nctx, nlat=nlat, reverse=reverse)

    @pl.when(s == 0)
    def _():
        h_ref[...] = jnp.zeros_like(h_ref)

    x = xbc_ref[...]
    rows = lax.broadcasted_iota(jnp.int32, x.shape, 0)
    has_prev = jnp.logical_and(c != 0, c != nctx).astype(F32)
    has_next = jnp.logical_and(c != nctx - 1, c != nctx + nlat - 1).astype(F32)
    xm1 = jnp.where(rows == 0, prev_ref[7:8, :] * has_prev, pltpu.roll(x, 1, 0))
    xp1 = jnp.where(rows == q - 1, next_ref[0:1, :] * has_next, pltpu.roll(x, q - 1, 0))
    conv = (convw_ref[0:1, :] * xm1 + convw_ref[1:2, :] * x + convw_ref[2:3, :] * xp1
            + convb_ref[...])
    act = _silu(conv)
    xs = act[:, :SSD_WIDTH]
    bm = act[:, SSD_WIDTH:SSD_WIDTH + SSD_GROUPS * SSD_STATE]
    cm = act[:, SSD_WIDTH + SSD_GROUPS * SSD_STATE:]

    dt = _softplus(dt_ref[...] + dtb_ref[...])
    a = dt * a_ref[...]
    li = lax.broadcasted_iota(jnp.int32, (q, q), 0)
    si = lax.broadcasted_iota(jnp.int32, (q, q), 1)
    mask = (si >= li) if reverse else (si <= li)
    cum = jnp.dot(mask.astype(F32), a, precision=HIGHEST, preferred_element_type=F32)
    cum_t = cum.T
    edge = 0 if reverse else q - 1

    hh = lax.broadcasted_iota(jnp.int32, (HEAD_LANES, SSD_WIDTH), 0)
    ll = lax.broadcasted_iota(jnp.int32, (HEAD_LANES, SSD_WIDTH), 1)
    expand = (lax.shift_right_logical(ll, int(math.log2(SSD_HEAD_DIM))) == hh).astype(F32)
    dt_e = jnp.dot(dt, expand, precision=HIGHEST, preferred_element_type=F32)
    cum_e = jnp.dot(cum, expand, precision=HIGHEST, preferred_element_type=F32)
    tot_e = cum_e[edge:edge + 1, :]
    xdt = xs * dt_e
    xdec = (xdt * jnp.exp(tot_e - cum_e)).astype(BF16)
    xdt_b = xdt.astype(BF16)
    off_scale = jnp.exp(cum_e)

    gw = SSD_WIDTH // SSD_GROUPS
    heads_per_group = SSD_HEADS // SSD_GROUPS
    lane = lax.broadcasted_iota(jnp.int32, (q, 2 * SSD_HEAD_DIM), 1)
    h_old = h_ref[...]
    y_groups = []
    st_groups = []
    for g in range(SSD_GROUPS):
        bg = bm[:, g * SSD_STATE:(g + 1) * SSD_STATE]
        cg = cm[:, g * SSD_STATE:(g + 1) * SSD_STATE].astype(BF16)
        cb = lax.dot_general(cg, bg.astype(BF16), (((1,), (1,)), ((), ())),
                             preferred_element_type=F32)
        pairs = []
        for pr in range(heads_per_group // 2):
            lo = g * gw + pr * 2 * SSD_HEAD_DIM
            xp = xdt_b[:, lo:lo + 2 * SSD_HEAD_DIM]
            ys = []
            for k in range(2):
                h = g * heads_per_group + pr * 2 + k
                seg = cum[:, h:h + 1] - cum_t[h:h + 1, :]
                lm = jnp.exp(jnp.where(mask, seg, NEG_INF))
                ys.append(jnp.dot((cb * lm).astype(BF16), xp, preferred_element_type=F32))
            pairs.append(jnp.where(lane < SSD_HEAD_DIM, ys[0], ys[1]))
        y_diag = jnp.concatenate(pairs, axis=1)
        hg = h_old[:, g * gw:(g + 1) * gw]
        y_off = jnp.dot(cg, hg.astype(BF16), preferred_element_type=F32)
        y_groups.append(y_diag + y_off * off_scale[:, g * gw:(g + 1) * gw])
        st_groups.append(jnp.dot(bg.T.astype(BF16), xdec[:, g * gw:(g + 1) * gw],
                                 preferred_element_type=F32))
    y = jnp.concatenate(y_groups, axis=1)
    h_ref[...] = h_old * jnp.exp(tot_e) + jnp.concatenate(st_groups, axis=1)

    if final:
        y = y + yprev_ref[...] + xs * dskip_ref[...]
        gated = y * _silu(z_ref[...])
        ms = jnp.mean(gated * gated, axis=-1, keepdims=True)
        out_ref[...] = gated * lax.rsqrt(ms + EPS) * ng_ref[...]
    else:
        out_ref[...] = y


def _ssd_pass(xbc, dt, z, yprev, convw, convb, a_row, dtb_row, dskip, ng, *,
              nb, ctx_len, seq, q, direction):
    t = xbc.shape[0]
    nctx, nlat = ctx_len // q, seq // q
    reverse = direction == 1
    final = direction == 1
    q8 = q // 8
    last8 = t // 8 - 1

    def blk(b, s):
        c = _ssd_chunk_index(s, nctx=nctx, nlat=nlat, reverse=reverse)
        return _ssd_row_block(b, c, nb=nb, nctx=nctx, nlat=nlat)

    row_map = lambda b, s: (blk(b, s), 0)
    prev_map = lambda b, s: (jnp.maximum(blk(b, s) * q8 - 1, 0), 0)
    next_map = lambda b, s: (jnp.minimum((blk(b, s) + 1) * q8, last8), 0)
    dt_map = lambda b, s: (blk(b, s), direction)
    const = lambda b, s: (0, 0)

    in_specs = [pl.BlockSpec((q, XBC_WIDTH), row_map),
                pl.BlockSpec((8, XBC_WIDTH), prev_map),
                pl.BlockSpec((8, XBC_WIDTH), next_map),
                pl.BlockSpec((q, HEAD_LANES), dt_map)]
    args = [xbc, xbc, xbc, dt]
    if final:
        in_specs += [pl.BlockSpec((q, SSD_WIDTH), row_map)] * 2
        args += [z, yprev]
    in_specs += [pl.BlockSpec(convw.shape, const), pl.BlockSpec(convb.shape, const),
                 pl.BlockSpec(a_row.shape, const), pl.BlockSpec(dtb_row.shape, const)]
    args += [convw, convb, a_row, dtb_row]
    if final:
        in_specs += [pl.BlockSpec(dskip.shape, const), pl.BlockSpec(ng.shape, const)]
        args += [dskip, ng]
    kern = functools.partial(_ssd_kernel, q=q, nctx=nctx, nlat=nlat,
                             reverse=reverse, final=final)
    return pl.pallas_call(
        kern,
        grid=(nb, nctx + nlat),
        in_specs=in_specs,
        out_specs=pl.BlockSpec((q, SSD_WIDTH), row_map),
        out_shape=jax.ShapeDtypeStruct((t, SSD_WIDTH), F32),
        scratch_shapes=[pltpu.VMEM((SSD_STATE, SSD_WIDTH), F32)],
        compiler_params=_cparams(("arbitrary", "arbitrary")),
        name="ssd_bwd" if final else "ssd_fwd",
    )(*args)


def _s5_block_index(i, *, nctx, nblk, reverse):
    if not reverse:
        return i
    return jnp.where(i < nctx, nctx - 1 - i, nblk - 1 - (i - nctx))


def _s5_kernel(*refs, steps, nb, reverse, final):
    if final:
        (u_ref, yprev_ref, bfull_ref, abar_ref, cfull_ref, dskip_ref, gluw_ref, glub_ref,
         out_ref, state_ref, x_ref) = refs
    else:
        u_ref, bfull_ref, abar_ref, cfull_ref, out_ref, state_ref, x_ref = refs

    @pl.when(pl.program_id(0) == 0)
    def _():
        state_ref[...] = jnp.zeros_like(state_ref)

    u = u_ref[...]
    x_ref[...] = jnp.dot(u.astype(BF16), bfull_ref[...], preferred_element_type=F32)
    n = abar_ref.shape[1] // 2
    half = n // 2
    for part in range(2):
        re = slice(part * half, (part + 1) * half)
        im = slice(n + part * half, n + (part + 1) * half)
        ar = abar_ref[:, re]
        ai = abar_ref[:, im]

        def step(k, carry, re=re, im=im, ar=ar, ai=ai):
            xr, xi = carry
            kk = steps - 1 - k if reverse else k
            r0 = pl.multiple_of(kk * nb, nb)
            nr = ar * xr - ai * xi + x_ref[pl.ds(r0, nb), re]
            ni = ar * xi + ai * xr + x_ref[pl.ds(r0, nb), im]
            x_ref[pl.ds(r0, nb), re] = nr
            x_ref[pl.ds(r0, nb), im] = ni
            return nr, ni

        xr, xi = lax.fori_loop(0, steps, step, (state_ref[:, re], state_ref[:, im]),
                               unroll=4)
        state_ref[:, re] = xr
        state_ref[:, im] = xi

    y = jnp.dot(x_ref[...].astype(BF16), cfull_ref[...], preferred_element_type=F32)
    if final:
        y = jax.nn.gelu(y + yprev_ref[...] + u * dskip_ref[...])
        gate = jnp.dot(y.astype(BF16), gluw_ref[...], preferred_element_type=F32)
        out_ref[...] = y * jax.nn.sigmoid(gate + glub_ref[...])
    else:
        out_ref[...] = y


def _s5_pass(u_scan, yprev, bfull, abar, cfull, dskip, gluw, glub, *,
             nb, ctx_len, steps, direction):
    rows, width = u_scan.shape
    nblk = rows // (steps * nb)
    nctx = ctx_len // steps
    reverse = direction == 1
    final = direction == 1
    row_map = lambda i: (_s5_block_index(i, nctx=nctx, nblk=nblk, reverse=reverse), 0)
    const = lambda i: (0, 0)
    blk = pl.BlockSpec((steps * nb, width), row_map)
    in_specs = [blk]
    args = [u_scan]
    if final:
        in_specs.append(blk)
        args.append(yprev)
    in_specs += [pl.BlockSpec(bfull.shape, const), pl.BlockSpec(abar.shape, const),
                 pl.BlockSpec(cfull.shape, const)]
    args += [bfull, abar, cfull]
    if final:
        in_specs += [pl.BlockSpec(dskip.shape, const), pl.BlockSpec(gluw.shape, const),
                     pl.BlockSpec(glub.shape, const)]
        args += [dskip, gluw, glub]
    kern = functools.partial(_s5_kernel, steps=steps, nb=nb, reverse=reverse, final=final)
    return pl.pallas_call(
        kern,
        grid=(nblk,),
        in_specs=in_specs,
        out_specs=blk,
        out_shape=jax.ShapeDtypeStruct((rows, width), F32),
        scratch_shapes=[pltpu.VMEM(abar.shape, F32),
                        pltpu.VMEM((steps * nb, abar.shape[1]), F32)],
        compiler_params=_cparams(("arbitrary",)),
        name="s5_bwd" if final else "s5_fwd",
    )(*args)


def _s5_params(lam_re, lam_im, log_step, b_re, b_im, c_re, c_im, nb):
    g, p = lam_re.shape[1], lam_re.shape[2]
    step = jnp.exp(log_step)[..., None]
    mag = jnp.exp(lam_re * step)
    ar = mag * jnp.cos(lam_im * step)
    ai = mag * jnp.sin(lam_im * step)
    den = lam_re * lam_re + lam_im * lam_im
    kr = ((ar - 1.0) * lam_re + ai * lam_im) / den
    ki = (ai * lam_re - (ar - 1.0) * lam_im) / den
    bbr = kr[..., None] * b_re - ki[..., None] * b_im
    bbi = kr[..., None] * b_im + ki[..., None] * b_re
    eye = jnp.eye(g, dtype=F32)
    cdim = b_re.shape[-1]

    def bmat(bb):
        return jnp.einsum('gpc,gh->gchp', bb, eye).reshape(g * cdim, g * p)

    bfull = jnp.stack([jnp.concatenate([bmat(bbr[d]), bmat(bbi[d])], axis=1)
                       for d in range(2)]).astype(BF16)
    abar = jnp.stack([jnp.broadcast_to(
        jnp.concatenate([ar[d].reshape(-1), ai[d].reshape(-1)])[None, :], (nb, 2 * g * p))
        for d in range(2)])

    def cmat(cc):
        return jnp.einsum('gcp,gh->gphc', cc, eye).reshape(g * p, g * cdim)

    cfull = jnp.concatenate([cmat(c_re), -cmat(c_im)], axis=0).astype(BF16)
    return bfull, abar, cfull


def _out_proj_kernel(x_ref, ssd_ref, s5_ref, mod_ref, w1_ref, w2_ref, g_ref, b_ref, o_ref, *,
                     d, alpha, n_ctx_tiles, lat_tiles, ctx_row):
    row = _mod_row(pl.program_id(0), n_ctx_tiles, lat_tiles, ctx_row)
    m2 = mod_ref[pl.ds(row, 1), 2 * d:3 * d]
    mix = (jnp.dot(ssd_ref[...].astype(BF16), w1_ref[...], preferred_element_type=F32)
           + jnp.dot(s5_ref[...].astype(BF16), w2_ref[...], preferred_element_type=F32))
    o_ref[...] = _layer_norm(alpha * x_ref[...] + m2 * mix, g_ref[...], b_ref[...])


def _out_proj(x, ssd, s5, mod, w1, w2, g, b, *, tm, alpha, n_ctx_tiles, lat_tiles, ctx_row):
    t, d = x.shape
    kern = functools.partial(_out_proj_kernel, d=d, alpha=alpha, n_ctx_tiles=n_ctx_tiles,
                             lat_tiles=lat_tiles, ctx_row=ctx_row)
    const = lambda i: (0, 0)
    return pl.pallas_call(
        kern,
        grid=(t // tm,),
        in_specs=[pl.BlockSpec((tm, d), lambda i: (i, 0)),
                  pl.BlockSpec((tm, ssd.shape[1]), lambda i: (i, 0)),
                  pl.BlockSpec((tm, s5.shape[1]), lambda i: (i, 0)),
                  pl.BlockSpec(mod.shape, const), pl.BlockSpec(w1.shape, const),
                  pl.BlockSpec(w2.shape, const), pl.BlockSpec(g.shape, const),
                  pl.BlockSpec(b.shape, const)],
        out_specs=pl.BlockSpec((tm, d), lambda i: (i, 0)),
        out_shape=jax.ShapeDtypeStruct((t, d), F32),
        compiler_params=_cparams(("arbitrary",)),
        name="out_proj_ln",
    )(x, ssd, s5, mod, w1, w2, g, b)


def _peer_kernel(x_ref, mod_ref, wqt_ref, keys_ref, u_ref, vt_ref, g_ref, b_ref, o_ref,
                 hbt_ref, acc_ref, q_ref, top_ref, cand_ref, ord_ref, work_ref, s0_ref, s1_ref,
                 rnk_ref, rank_ref, e1_ref, n_ref, r_ref, *,
                 d, nk, heads, half, alpha, n_ctx_tiles, lat_tiles, ctx_row):
    i = pl.program_id(0)
    j = pl.program_id(1)
    tm = x_ref.shape[0]
    eb = u_ref.shape[0]
    nsel = PEER_TOPK + 1
    pad_rows = top_ref.shape[1]
    pk = 16

    @pl.when(j == 0)
    def _prepare():
        row = _mod_row(i, n_ctx_tiles, lat_tiles, ctx_row)
        m3 = mod_ref[pl.ds(row, 1), 3 * d:4 * d]
        m4 = mod_ref[pl.ds(row, 1), 4 * d:5 * d]
        h = x_ref[...] * (1.0 + m4) + m3
        hbt = h.T.astype(BF16)
        hbt_ref[...] = hbt
        q_ref[...] = jnp.dot(wqt_ref[...], hbt, preferred_element_type=F32)
        acc_ref[...] = jnp.zeros_like(acc_ref)

        def per_head(hd, carry):
            top_ref[...] = jnp.full(top_ref.shape, NEG_INF, F32)
            rnk_ref[...] = jnp.full(rnk_ref.shape, 2.0 * nsel, F32)
            for c, dst in ((0, s0_ref), (1, s1_ref)):
                off = pl.multiple_of((hd * 2 + c) * nk, nk)
                qoff = pl.multiple_of((hd * 2 + c) * half, half)
                sc = jnp.dot(keys_ref[pl.ds(off, nk), :], q_ref[pl.ds(qoff, half), :],
                             precision=HIGHEST, preferred_element_type=F32)
                dst[...] = sc
                work_ref[c] = sc

            def take(r, carry):
                for c in range(2):
                    cur = work_ref[c]
                    m = jnp.max(cur, axis=0, keepdims=True)
                    top_ref[c, pl.ds(r, 1), :] = m
                    hit = cur >= m
                    work_ref[c] = jnp.where(hit, NEG_INF, cur)
                    if c == 1:
                        rnk_ref[...] = jnp.where(hit, (r + 1).astype(F32), rnk_ref[...])
                return carry

            lax.fori_loop(0, nsel, take, 0)
            v0 = top_ref[0]
            v1 = top_ref[1]
            cands = [v0[0:1, :] + v1]
            cands += [v0[a:a + 1, :] + v1[0:8, :] for a in range(1, 8)]
            cands += [v0[8:pad_rows, :] + v1[0:1, :]]
            cand = jnp.concatenate(cands, axis=0)
            cand_ref[...] = cand

            def take2(r, carry):
                cur = cand_ref[...]
                m = jnp.max(cur, axis=0, keepdims=True)
                ord_ref[pl.ds(r, 1), :] = m
                cand_ref[...] = jnp.where(cur >= m, NEG_INF, cur)
                return carry

            lax.fori_loop(0, nsel, take2, 0)
            tau = 0.5 * (ord_ref[nsel - 2:nsel - 1, :] + ord_ref[nsel - 1:nsel, :])
            top = v0[0:1, :] + v1[0:1, :]
            z = jnp.sum(jnp.where(cand > tau, jnp.exp(cand - top), 0.0), axis=0, keepdims=True)
            s0 = s0_ref[...]
            theta = tau - s0
            count = jnp.zeros_like(theta)
            for bb in range(PEER_TOPK):
                count = count + (v1[bb:bb + 1, :] >= theta).astype(F32)
            n_ref[hd] = count
            r_ref[hd] = jnp.exp(s0 - v0[0:1, :]) / z
            e1 = jnp.exp(s1_ref[...] - v1[0:1, :])
            e1_ref[hd] = e1.astype(BF16).reshape(nk // pk, pk, tm)
            rank_ref[hd] = rnk_ref[...].astype(BF16).reshape(nk // pk, pk, tm)
            return carry

        lax.fori_loop(0, heads, per_head, 0)

    lanes = 128
    keys_per_step = eb // nk
    sub_keys = 4
    base = pl.multiple_of(j * keys_per_step, keys_per_step)
    hbt = hbt_ref[...]
    sub_rows = sub_keys * nk
    n_sub = keys_per_step // sub_keys

    def expert_acts(sub):
        return jnp.dot(u_ref[sub * sub_rows:(sub + 1) * sub_rows, :], hbt,
                       preferred_element_type=F32)

    def project(sub, gate):
        acc_ref[...] += jnp.dot(vt_ref[:, sub * sub_rows:(sub + 1) * sub_rows], gate,
                                preferred_element_type=F32)

    a_next = expert_acts(0)
    gate_prev = None
    for sub in range(n_sub):
        a_sub = a_next
        if sub + 1 < n_sub:
            a_next = expert_acts(sub + 1)
        if gate_prev is not None:
            project(sub - 1, gate_prev)
        rows = []
        for k2 in range(sub_keys):
            ii = sub * sub_keys + k2
            cols = []
            for lt in range(tm // lanes):
                ls = slice(lt * lanes, (lt + 1) * lanes)
                w = jnp.zeros((nk // pk, pk, lanes), BF16)
                for hd in range(heads):
                    cnt = n_ref[hd, pl.ds(base, keys_per_step), ls][ii:ii + 1, :]
                    rr = r_ref[hd, pl.ds(base, keys_per_step), ls][ii:ii + 1, :]
                    cnt = jnp.broadcast_to(cnt, (pk, lanes)).astype(BF16)[None]
                    rr = jnp.broadcast_to(rr, (pk, lanes)).astype(BF16)[None]
                    w = w + jnp.where(rank_ref[hd, :, :, ls] <= cnt, e1_ref[hd, :, :, ls],
                                      jnp.zeros((), BF16)) * rr
                a = a_sub[k2 * nk:(k2 + 1) * nk, ls].astype(BF16).reshape(nk // pk, pk, lanes)
                cols.append((w * jax.nn.gelu(a)).reshape(nk, lanes))
            rows.append(jnp.concatenate(cols, axis=1))
        gate_prev = jnp.concatenate(rows, axis=0)
    project(n_sub - 1, gate_prev)

    @pl.when(j == pl.num_programs(1) - 1)
    def _finish():
        row = _mod_row(i, n_ctx_tiles, lat_tiles, ctx_row)
        m5 = mod_ref[pl.ds(row, 1), 5 * d:6 * d]
        f = acc_ref[...].T
        o_ref[...] = _layer_norm(alpha * x_ref[...] + m5 * f, g_ref[...], b_ref[...])


def _peer(x, mod, wqt, keys2d, u_b, vt_b, g, b, *, tm, eb, heads, alpha,
          n_ctx_tiles, lat_tiles, ctx_row):
    t, d = x.shape
    nexp = u_b.shape[0]
    nk = keys2d.shape[0] // (2 * heads)
    half = keys2d.shape[1]
    pad_rows = 24
    ncand = pad_rows + 7 * 8 + (pad_rows - 8)
    kern = functools.partial(_peer_kernel, d=d, nk=nk, heads=heads, half=half, alpha=alpha,
                             n_ctx_tiles=n_ctx_tiles, lat_tiles=lat_tiles, ctx_row=ctx_row)
    const = lambda i, j: (0, 0)
    return pl.pallas_call(
        kern,
        grid=(t // tm, nexp // eb),
        in_specs=[pl.BlockSpec((tm, d), lambda i, j: (i, 0)),
                  pl.BlockSpec(mod.shape, const), pl.BlockSpec(wqt.shape, const),
                  pl.BlockSpec(keys2d.shape, const),
                  pl.BlockSpec((eb, d), lambda i, j: (j, 0)),
                  pl.BlockSpec((d, eb), lambda i, j: (0, j)),
                  pl.BlockSpec(g.shape, const), pl.BlockSpec(b.shape, const)],
        out_specs=pl.BlockSpec((tm, d), lambda i, j: (i, 0)),
        out_shape=jax.ShapeDtypeStruct((t, d), F32),
        scratch_shapes=[pltpu.VMEM((d, tm), BF16),
                        pltpu.VMEM((d, tm), F32),
                        pltpu.VMEM((wqt.shape[0], tm), F32),
                        pltpu.VMEM((2, pad_rows, tm), F32),
                        pltpu.VMEM((ncand, tm), F32),
                        pltpu.VMEM((pad_rows, tm), F32),
                        pltpu.VMEM((2, nk, tm), F32),
                        pltpu.VMEM((nk, tm), F32),
                        pltpu.VMEM((nk, tm), F32),
                        pltpu.VMEM((nk, tm), F32),
                        pltpu.VMEM((heads, nk // 16, 16, tm), BF16),
                        pltpu.VMEM((heads, nk // 16, 16, tm), BF16),
                        pltpu.VMEM((heads, nk, tm), F32),
                        pltpu.VMEM((heads, nk, tm), F32)],
        compiler_params=_cparams(("arbitrary", "arbitrary")),
        name="peer_ln",
    )(x, mod, wqt, keys2d, u_b, vt_b, g, b)


def _pick_tile(limit, *sizes):
    tm = limit
    while any(s % tm for s in sizes):
        tm //= 2
    return tm


def kernel(x, c, ctx, c_ctx, w_mod, b_mod, w_in, conv_w, conv_b, ssd_a_log, ssd_dt_bias, ssd_d, ssd_norm_g, s5_lam_re, s5_lam_im, s5_log_step, s5_b_re, s5_b_im, s5_c_re, s5_c_im, s5_d, s5_glu_w, s5_glu_b, w_out, ln1_g, ln1_b, peer_wq, peer_keys, peer_u, peer_v, ln2_g, ln2_b):
    nb, seq, d = x.shape
    ctx_len = ctx.shape[1]
    depth = w_mod.shape[0]
    s5_width = s5_d.shape[1]
    heads = peer_keys.shape[1]
    nk = peer_keys.shape[3]
    alpha = (2.0 * depth) ** 0.25
    rows_lat = seq // GRID_W

    tm = _pick_tile(256, nb * ctx_len, seq)
    tp = _pick_tile(512, nb * ctx_len, seq)
    q = _pick_tile(128, ctx_len, seq)
    steps = _pick_tile(64, ctx_len, seq)
    eb = 16 * nk
    ctx_row = nb

    mod_rows = -(-(nb + 1) // 8) * 8
    cc = jnp.concatenate([c, c_ctx[None, :], jnp.zeros((mod_rows - nb - 1, d), F32)], axis=0)
    mods = _mod_table(cc, w_mod, b_mod)

    xt = jnp.concatenate([ctx.reshape(nb * ctx_len, d), x.reshape(nb * seq, d)], axis=0)

    def pad_lanes(w, n):
        return jnp.pad(w, ((0, 0), (0, n - w.shape[1])))

    for i in range(depth):
        wi = w_in[i]
        o_dt = SSD_WIDTH + XBC_WIDTH
        w_pad = jnp.concatenate(
            [wi[:, :o_dt],
             pad_lanes(wi[:, o_dt:o_dt + SSD_HEADS], HEAD_LANES),
             pad_lanes(wi[:, o_dt + SSD_HEADS:o_dt + 2 * SSD_HEADS], HEAD_LANES),
             wi[:, o_dt + 2 * SSD_HEADS:]], axis=1).astype(BF16)
        a_rows = pad_lanes(-jnp.exp(ssd_a_log[i]), HEAD_LANES)
        dtb_rows = pad_lanes(ssd_dt_bias[i], HEAD_LANES)
        dskip_ssd = jnp.repeat(ssd_d[i], SSD_HEAD_DIM)[None, :]
        bfull, abar, cfull = _s5_params(s5_lam_re[i], s5_lam_im[i], s5_log_step[i],
                                        s5_b_re[i], s5_b_im[i], s5_c_re[i], s5_c_im[i], nb)

        z, xbc, dt, u = _in_proj(xt, mods[i], w_pad, tm=tm, n_ctx_tiles=nb * ctx_len // tm,
                                 lat_tiles=seq // tm, ctx_row=ctx_row, s5_width=s5_width)
        ssd_args = dict(nb=nb, ctx_len=ctx_len, seq=seq, q=q)
        y_f = _ssd_pass(xbc, dt, None, None, conv_w[i], conv_b[i][None, :], a_rows[0:1],
                        dtb_rows[0:1], None, None, direction=0, **ssd_args)
        ssd_out = _ssd_pass(xbc, dt, z, y_f, conv_w[i], conv_b[i][None, :], a_rows[1:2],
                            dtb_rows[1:2], dskip_ssd, ssd_norm_g[i][None, :], direction=1,
                            **ssd_args)

        u_ctx = u[:nb * ctx_len].reshape(nb, ctx_len, s5_width).transpose(1, 0, 2)
        u_lat = u[nb * ctx_len:].reshape(nb, rows_lat, GRID_W, s5_width).transpose(2, 1, 0, 3)
        u_scan = jnp.concatenate([u_ctx.reshape(ctx_len * nb, s5_width),
                                  u_lat.reshape(seq * nb, s5_width)], axis=0)
        s5_args = dict(nb=nb, ctx_len=ctx_len, steps=steps)
        ys_f = _s5_pass(u_scan, None, bfull[0], abar[0], cfull, None, None, None,
                        direction=0, **s5_args)
        s5_scan = _s5_pass(u_scan, ys_f, bfull[1], abar[1], cfull, s5_d[i][None, :],
                           s5_glu_w[i].astype(BF16), s5_glu_b[i][None, :], direction=1,
                           **s5_args)
        s5_ctx = s5_scan[:ctx_len * nb].reshape(ctx_len, nb, s5_width).transpose(1, 0, 2)
        s5_lat = s5_scan[ctx_len * nb:].reshape(GRID_W, rows_lat, nb, s5_width)
        s5_lat = s5_lat.transpose(2, 1, 0, 3)
        s5_out = jnp.concatenate([s5_ctx.reshape(nb * ctx_len, s5_width),
                                  s5_lat.reshape(nb * seq, s5_width)], axis=0)

        wo = w_out[i].astype(BF16)
        xt = _out_proj(xt, ssd_out, s5_out, mods[i], wo[:SSD_WIDTH], wo[SSD_WIDTH:],
                       ln1_g[i][None, :], ln1_b[i][None, :], tm=tm, alpha=alpha,
                       n_ctx_tiles=nb * ctx_len // tm, lat_tiles=seq // tm, ctx_row=ctx_row)

        xt = _peer(xt, mods[i], peer_wq[i].T.astype(BF16),
                   peer_keys[i].reshape(heads * 2 * nk, -1),
                   peer_u[i].astype(BF16), peer_v[i].T.astype(BF16),
                   ln2_g[i][None, :], ln2_b[i][None, :], tm=tp, eb=eb, heads=heads,
                   alpha=alpha, n_ctx_tiles=nb * ctx_len // tp, lat_tiles=seq // tp,
                   ctx_row=ctx_row)

    return xt[nb * ctx_len:].reshape(nb, seq, d)
```

```python
import functools
import math

import jax
import jax.numpy as jnp
from jax import lax
from jax.experimental import pallas as pl
from jax.experimental.pallas import tpu as pltpu

GRID_W = 64

SSD_HEADS = 12
SSD_HEAD_DIM = 64
SSD_GROUPS = 2
SSD_STATE = 128
SSD_WIDTH = SSD_HEADS * SSD_HEAD_DIM
XBC_WIDTH = SSD_WIDTH + 2 * SSD_GROUPS * SSD_STATE
HEAD_LANES = 128
S5_GROUP = 16
S5_STATE = 64
PEER_TOPK = 16
EPS = 1e-5

F32 = jnp.float32
BF16 = jnp.bfloat16
HIGHEST = lax.Precision.HIGHEST
NEG_INF = float("-inf")

VMEM_LIMIT = 56 * 1024 * 1024


def _cparams(sem):
    return pltpu.CompilerParams(dimension_semantics=sem, vmem_limit_bytes=VMEM_LIMIT)


def _silu(x):
    return x * jax.nn.sigmoid(x)


def _softplus(x):
    return jnp.maximum(x, 0.0) + jnp.log1p(jnp.exp(-jnp.abs(x)))


def _layer_norm(y, g, b):
    mu = jnp.mean(y, axis=-1, keepdims=True)
    d = y - mu
    var = jnp.mean(d * d, axis=-1, keepdims=True)
    return d * lax.rsqrt(var + EPS) * g + b


def _mod_row(tile, n_ctx_tiles, lat_tiles_per_batch, ctx_row):
    return jnp.where(tile < n_ctx_tiles, ctx_row,
                     (tile - n_ctx_tiles) // lat_tiles_per_batch)


def _mod_kernel(c_ref, w_ref, b_ref, o_ref):
    o_ref[...] = jnp.dot(_silu(c_ref[...]), w_ref[...], precision=HIGHEST,
                         preferred_element_type=F32) + b_ref[...]


def _mod_table(cc, w_mod, b_mod):
    depth, d, d6 = w_mod.shape
    rows = cc.shape[0]
    nj = d6 // d
    return pl.pallas_call(
        _mod_kernel,
        grid=(depth, nj),
        in_specs=[pl.BlockSpec((rows, d), lambda l, j: (0, 0)),
                  pl.BlockSpec((None, d, d), lambda l, j: (l, 0, j)),
                  pl.BlockSpec((None, 1, d), lambda l, j: (l, 0, j))],
        out_specs=pl.BlockSpec((None, rows, d), lambda l, j: (l, 0, j)),
        out_shape=jax.ShapeDtypeStruct((depth, rows, d6), F32),
        compiler_params=_cparams(("arbitrary", "arbitrary")),
        name="mod_table",
    )(cc, w_mod, b_mod.reshape(depth, 1, d6))


def _in_proj_kernel(x_ref, mod_ref, w_ref, z_ref, xbc_ref, dt_ref, u_ref, *,
                    d, n_ctx_tiles, lat_tiles, ctx_row):
    row = _mod_row(pl.program_id(0), n_ctx_tiles, lat_tiles, ctx_row)
    m0 = mod_ref[pl.ds(row, 1), 0:d]
    m1 = mod_ref[pl.ds(row, 1), d:2 * d]
    xm = (x_ref[...] * (1.0 + m1) + m0).astype(BF16)
    p = jnp.dot(xm, w_ref[...], preferred_element_type=F32)
    o = 0
    for ref in (z_ref, xbc_ref, dt_ref, u_ref):
        w = ref.shape[-1]
        ref[...] = p[:, o:o + w]
        o += w


def _in_proj(x, mod, w, *, tm, n_ctx_tiles, lat_tiles, ctx_row, s5_width):
    t, d = x.shape
    widths = (SSD_WIDTH, XBC_WIDTH, 2 * HEAD_LANES, s5_width)
    kern = functools.partial(_in_proj_kernel, d=d, n_ctx_tiles=n_ctx_tiles,
                             lat_tiles=lat_tiles, ctx_row=ctx_row)
    return pl.pallas_call(
        kern,
        grid=(t // tm,),
        in_specs=[pl.BlockSpec((tm, d), lambda i: (i, 0)),
                  pl.BlockSpec(mod.shape, lambda i: (0, 0)),
                  pl.BlockSpec(w.shape, lambda i: (0, 0))],
        out_specs=[pl.BlockSpec((tm, wd), lambda i: (i, 0)) for wd in widths],
        out_shape=[jax.ShapeDtypeStruct((t, wd), F32) for wd in widths],
        compiler_params=_cparams(("arbitrary",)),
        name="in_proj",
    )(x, mod, w)


def _ssd_chunk_index(s, *, nctx, nlat, reverse):
    if not reverse:
        return s
    return jnp.where(s < nctx, nctx - 1 - s, nctx + nlat - 1 - (s - nctx))


def _ssd_row_block(b, c, *, nb, nctx, nlat):
    return jnp.where(c < nctx, b * nctx + c, nb * nctx + b * nlat + (c - nctx))


def _ssd_kernel(*refs, q, nctx, nlat, reverse, final):
    if final:
        (xbc_ref, prev_ref, next_ref, dt_ref, z_ref, yprev_ref, convw_ref, convb_ref,
         a_ref, dtb_ref, dskip_ref, ng_ref, out_ref, h_ref) = refs
    else:
        (xbc_ref, prev_ref, next_ref, dt_ref, convw_ref, convb_ref,
         a_ref, dtb_ref, out_ref, h_ref) = refs
    s = pl.program_id(1)
    c = _ssd_chunk_index(s, nctx=nctx, nlat=nlat, reverse=reverse)

    @pl.when(s == 0)
    def _():
        h_ref[...] = jnp.zeros_like(h_ref)

    x = xbc_ref[...]
    rows = lax.broadcasted_iota(jnp.int32, x.shape, 0)
    has_prev = jnp.logical_and(c != 0, c != nctx).astype(F32)
    has_next = jnp.logical_and(c != nctx - 1, c != nctx + nlat - 1).astype(F32)
    xm1 = jnp.where(rows == 0, prev_ref[7:8, :] * has_prev, pltpu.roll(x, 1, 0))
    xp1 = jnp.where(rows == q - 1, next_ref[0:1, :] * has_next, pltpu.roll(x, q - 1, 0))
    conv = (convw_ref[0:1, :] * xm1 + convw_ref[1:2, :] * x + convw_ref[2:3, :] * xp1
            + convb_ref[...])
    act = _silu(conv)
    xs = act[:, :SSD_WIDTH]
    bm = act[:, SSD_WIDTH:SSD_WIDTH + SSD_GROUPS * SSD_STATE]
    cm = act[:, SSD_WIDTH + SSD_GROUPS * SSD_STATE:]

    dt = _softplus(dt_ref[...] + dtb_ref[...])
    a = dt * a_ref[...]
    li = lax.broadcasted_iota(jnp.int32, (q, q), 0)
    si = lax.broadcasted_iota(jnp.int32, (q, q), 1)
    mask = (si >= li) if reverse else (si <= li)
    cum = jnp.dot(mask.astype(F32), a, precision=HIGHEST, preferred_element_type=F32)
    cum_t = cum.T
    edge = 0 if reverse else q - 1

    hh = lax.broadcasted_iota(jnp.int32, (HEAD_LANES, SSD_WIDTH), 0)
    ll = lax.broadcasted_iota(jnp.int32, (HEAD_LANES, SSD_WIDTH), 1)
    expand = (lax.shift_right_logical(ll, int(math.log2(SSD_HEAD_DIM))) == hh).astype(F32)
    dt_e = jnp.dot(dt, expand, precision=HIGHEST, preferred_element_type=F32)
    cum_e = jnp.dot(cum, expand, precision=HIGHEST, preferred_element_type=F32)
    tot_e = cum_e[edge:edge + 1, :]
    xdt = xs * dt_e
    xdec = (xdt * jnp.exp(tot_e - cum_e)).astype(BF16)
    xdt_b = xdt.astype(BF16)
    off_scale = jnp.exp(cum_e)

    gw = SSD_WIDTH // SSD_GROUPS
    heads_per_group = SSD_HEADS // SSD_GROUPS
    lane = lax.broadcasted_iota(jnp.int32, (q, 2 * SSD_HEAD_DIM), 1)
    h_old = h_ref[...]
    y_groups = []
    st_groups = []
    for g in range(SSD_GROUPS):
        bg = bm[:, g * SSD_STATE:(g + 1) * SSD_STATE]
        cg = cm[:, g * SSD_STATE:(g + 1) * SSD_STATE].astype(BF16)
        cb = lax.dot_general(cg, bg.astype(BF16), (((1,), (1,)), ((), ())),
                             preferred_element_type=F32)
        pairs = []
        for pr in range(heads_per_group // 2):
            lo = g * gw + pr * 2 * SSD_HEAD_DIM
            xp = xdt_b[:, lo:lo + 2 * SSD_HEAD_DIM]
            ys = []
            for k in range(2):
                h = g * heads_per_group + pr * 2 + k
                seg = cum[:, h:h + 1] - cum_t[h:h + 1, :]
                lm = jnp.exp(jnp.where(mask, seg, NEG_INF))
                ys.append(jnp.dot((cb * lm).astype(BF16), xp, preferred_element_type=F32))
            pairs.append(jnp.where(lane < SSD_HEAD_DIM, ys[0], ys[1]))
        y_diag = jnp.concatenate(pairs, axis=1)
        hg = h_old[:, g * gw:(g + 1) * gw]
        y_off = jnp.dot(cg, hg.astype(BF16), preferred_element_type=F32)
        y_groups.append(y_diag + y_off * off_scale[:, g * gw:(g + 1) * gw])
        st_groups.append(jnp.dot(bg.T.astype(BF16), xdec[:, g * gw:(g + 1) * gw],
                                 preferred_element_type=F32))
    y = jnp.concatenate(y_groups, axis=1)
    h_ref[...] = h_old * jnp.exp(tot_e) + jnp.concatenate(st_groups, axis=1)

    if final:
        y = y + yprev_ref[...] + xs * dskip_ref[...]
        gated = y * _silu(z_ref[...])
        ms = jnp.mean(gated * gated, axis=-1, keepdims=True)
        out_ref[...] = gated * lax.rsqrt(ms + EPS) * ng_ref[...]
    else:
        out_ref[...] = y


def _ssd_pass(xbc, dt, z, yprev, convw, convb, a_row, dtb_row, dskip, ng, *,
              nb, ctx_len, seq, q, direction):
    t = xbc.shape[0]
    nctx, nlat = ctx_len // q, seq // q
    reverse = direction == 1
    final = direction == 1
    q8 = q // 8
    last8 = t // 8 - 1

    def blk(b, s):
        c = _ssd_chunk_index(s, nctx=nctx, nlat=nlat, reverse=reverse)
        return _ssd_row_block(b, c, nb=nb, nctx=nctx, nlat=nlat)

    row_map = lambda b, s: (blk(b, s), 0)
    prev_map = lambda b, s: (jnp.maximum(blk(b, s) * q8 - 1, 0), 0)
    next_map = lambda b, s: (jnp.minimum((blk(b, s) + 1) * q8, last8), 0)
    dt_map = lambda b, s: (blk(b, s), direction)
    const = lambda b, s: (0, 0)

    in_specs = [pl.BlockSpec((q, XBC_WIDTH), row_map),
                pl.BlockSpec((8, XBC_WIDTH), prev_map),
                pl.BlockSpec((8, XBC_WIDTH), next_map),
                pl.BlockSpec((q, HEAD_LANES), dt_map)]
    args = [xbc, xbc, xbc, dt]
    if final:
        in_specs += [pl.BlockSpec((q, SSD_WIDTH), row_map)] * 2
        args += [z, yprev]
    in_specs += [pl.BlockSpec(convw.shape, const), pl.BlockSpec(convb.shape, const),
                 pl.BlockSpec(a_row.shape, const), pl.BlockSpec(dtb_row.shape, const)]
    args += [convw, convb, a_row, dtb_row]
    if final:
        in_specs += [pl.BlockSpec(dskip.shape, const), pl.BlockSpec(ng.shape, const)]
        args += [dskip, ng]
    kern = functools.partial(_ssd_kernel, q=q, nctx=nctx, nlat=nlat,
                             reverse=reverse, final=final)
    return pl.pallas_call(
        kern,
        grid=(nb, nctx + nlat),
        in_specs=in_specs,
        out_specs=pl.BlockSpec((q, SSD_WIDTH), row_map),
        out_shape=jax.ShapeDtypeStruct((t, SSD_WIDTH), F32),
        scratch_shapes=[pltpu.VMEM((SSD_STATE, SSD_WIDTH), F32)],
        compiler_params=_cparams(("arbitrary", "arbitrary")),
        name="ssd_bwd" if final else "ssd_fwd",
    )(*args)


def _s5_block_index(i, *, nctx, nblk, reverse):
    if not reverse:
        return i
    return jnp.where(i < nctx, nctx - 1 - i, nblk - 1 - (i - nctx))


def _s5_kernel(*refs, steps, nb, reverse, final):
    if final:
        (u_ref, yprev_ref, bfull_ref, abar_ref, cfull_ref, dskip_ref, gluw_ref, glub_ref,
         out_ref, state_ref, x_ref) = refs
    else:
        u_ref, bfull_ref, abar_ref, cfull_ref, out_ref, state_ref, x_ref = refs

    @pl.when(pl.program_id(0) == 0)
    def _():
        state_ref[...] = jnp.zeros_like(state_ref)

    u = u_ref[...]
    x_ref[...] = jnp.dot(u.astype(BF16), bfull_ref[...], preferred_element_type=F32)
    n = abar_ref.shape[1] // 2
    half = n // 2
    for part in range(2):
        re = slice(part * half, (part + 1) * half)
        im = slice(n + part * half, n + (part + 1) * half)
        ar = abar_ref[:, re]
        ai = abar_ref[:, im]

        def step(k, carry, re=re, im=im, ar=ar, ai=ai):
            xr, xi = carry
            kk = steps - 1 - k if reverse else k
            r0 = pl.multiple_of(kk * nb, nb)
            nr = ar * xr - ai * xi + x_ref[pl.ds(r0, nb), re]
            ni = ar * xi + ai * xr + x_ref[pl.ds(r0, nb), im]
            x_ref[pl.ds(r0, nb), re] = nr
            x_ref[pl.ds(r0, nb), im] = ni
            return nr, ni

        xr, xi = lax.fori_loop(0, steps, step, (state_ref[:, re], state_ref[:, im]),
                               unroll=4)
        state_ref[:, re] = xr
        state_ref[:, im] = xi

    y = jnp.dot(x_ref[...].astype(BF16), cfull_ref[...], preferred_element_type=F32)
    if final:
        y = jax.nn.gelu(y + yprev_ref[...] + u * dskip_ref[...])
        gate = jnp.dot(y.astype(BF16), gluw_ref[...], preferred_element_type=F32)
        out_ref[...] = y * jax.nn.sigmoid(gate + glub_ref[...])
    else:
        out_ref[...] = y


def _s5_pass(u_scan, yprev, bfull, abar, cfull, dskip, gluw, glub, *,
             nb, ctx_len, steps, direction):
    rows, width = u_scan.shape
    nblk = rows // (steps * nb)
    nctx = ctx_len // steps
    reverse = direction == 1
    final = direction == 1
    row_map = lambda i: (_s5_block_index(i, nctx=nctx, nblk=nblk, reverse=reverse), 0)
    const = lambda i: (0, 0)
    blk = pl.BlockSpec((steps * nb, width), row_map)
    in_specs = [blk]
    args = [u_scan]
    if final:
        in_specs.append(blk)
        args.append(yprev)
    in_specs += [pl.BlockSpec(bfull.shape, const), pl.BlockSpec(abar.shape, const),
                 pl.BlockSpec(cfull.shape, const)]
    args += [bfull, abar, cfull]
    if final:
        in_specs += [pl.BlockSpec(dskip.shape, const), pl.BlockSpec(gluw.shape, const),
                     pl.BlockSpec(glub.shape, const)]
        args += [dskip, gluw, glub]
    kern = functools.partial(_s5_kernel, steps=steps, nb=nb, reverse=reverse, final=final)
    return pl.pallas_call(
        kern,
        grid=(nblk,),
        in_specs=in_specs,
        out_specs=blk,
        out_shape=jax.ShapeDtypeStruct((rows, width), F32),
        scratch_shapes=[pltpu.VMEM(abar.shape, F32),
                        pltpu.VMEM((steps * nb, abar.shape[1]), F32)],
        compiler_params=_cparams(("arbitrary",)),
        name="s5_bwd" if final else "s5_fwd",
    )(*args)


def _s5_params(lam_re, lam_im, log_step, b_re, b_im, c_re, c_im, nb):
    g, p = lam_re.shape[1], lam_re.shape[2]
    step = jnp.exp(log_step)[..., None]
    mag = jnp.exp(lam_re * step)
    ar = mag * jnp.cos(lam_im * step)
    ai = mag * jnp.sin(lam_im * step)
    den = lam_re * lam_re + lam_im * lam_im
    kr = ((ar - 1.0) * lam_re + ai * lam_im) / den
    ki = (ai * lam_re - (ar - 1.0) * lam_im) / den
    bbr = kr[..., None] * b_re - ki[..., None] * b_im
    bbi = kr[..., None] * b_im + ki[..., None] * b_re
    eye = jnp.eye(g, dtype=F32)
    cdim = b_re.shape[-1]

    def bmat(bb):
        return jnp.einsum('gpc,gh->gchp', bb, eye).reshape(g * cdim, g * p)

    bfull = jnp.stack([jnp.concatenate([bmat(bbr[d]), bmat(bbi[d])], axis=1)
                       for d in range(2)]).astype(BF16)
    abar = jnp.stack([jnp.broadcast_to(
        jnp.concatenate([ar[d].reshape(-1), ai[d].reshape(-1)])[None, :], (nb, 2 * g * p))
        for d in range(2)])

    def cmat(cc):
        return jnp.einsum('gcp,gh->gphc', cc, eye).reshape(g * p, g * cdim)

    cfull = jnp.concatenate([cmat(c_re), -cmat(c_im)], axis=0).astype(BF16)
    return bfull, abar, cfull


def _out_proj_kernel(x_ref, ssd_ref, s5_ref, mod_ref, w1_ref, w2_ref, g_ref, b_ref, o_ref, *,
                     d, alpha, n_ctx_tiles, lat_tiles, ctx_row):
    row = _mod_row(pl.program_id(0), n_ctx_tiles, lat_tiles, ctx_row)
    m2 = mod_ref[pl.ds(row, 1), 2 * d:3 * d]
    mix = (jnp.dot(ssd_ref[...].astype(BF16), w1_ref[...], preferred_element_type=F32)
           + jnp.dot(s5_ref[...].astype(BF16), w2_ref[...], preferred_element_type=F32))
    o_ref[...] = _layer_norm(alpha * x_ref[...] + m2 * mix, g_ref[...], b_ref[...])


def _out_proj(x, ssd, s5, mod, w1, w2, g, b, *, tm, alpha, n_ctx_tiles, lat_tiles, ctx_row):
    t, d = x.shape
    kern = functools.partial(_out_proj_kernel, d=d, alpha=alpha, n_ctx_tiles=n_ctx_tiles,
                             lat_tiles=lat_tiles, ctx_row=ctx_row)
    const = lambda i: (0, 0)
    return pl.pallas_call(
        kern,
        grid=(t // tm,),
        in_specs=[pl.BlockSpec((tm, d), lambda i: (i, 0)),
                  pl.BlockSpec((tm, ssd.shape[1]), lambda i: (i, 0)),
                  pl.BlockSpec((tm, s5.shape[1]), lambda i: (i, 0)),
                  pl.BlockSpec(mod.shape, const), pl.BlockSpec(w1.shape, const),
                  pl.BlockSpec(w2.shape, const), pl.BlockSpec(g.shape, const),
                  pl.BlockSpec(b.shape, const)],
        out_specs=pl.BlockSpec((tm, d), lambda i: (i, 0)),
        out_shape=jax.ShapeDtypeStruct((t, d), F32),
        compiler_params=_cparams(("arbitrary",)),
        name="out_proj_ln",
    )(x, ssd, s5, mod, w1, w2, g, b)


def _peer_kernel(x_ref, mod_ref, wqt_ref, keys_ref, u_ref, vt_ref, g_ref, b_ref, o_ref,
                 hbt_ref, acc_ref, top_ref, cand_ref, ord_ref, work_ref, s0_ref, s1_ref,
                 rnk_ref, rank_ref, e1_ref, n_ref, r_ref, *,
                 d, nk, heads, half, alpha, n_ctx_tiles, lat_tiles, ctx_row):
    i = pl.program_id(0)
    j = pl.program_id(1)
    tm = x_ref.shape[0]
    eb = u_ref.shape[0]
    nsel = PEER_TOPK + 1
    pad_rows = top_ref.shape[1]
    pk = 16

    @pl.when(j == 0)
    def _prepare():
        row = _mod_row(i, n_ctx_tiles, lat_tiles, ctx_row)
        m3 = mod_ref[pl.ds(row, 1), 3 * d:4 * d]
        m4 = mod_ref[pl.ds(row, 1), 4 * d:5 * d]
        h = x_ref[...] * (1.0 + m4) + m3
        hbt = h.T.astype(BF16)
        hbt_ref[...] = hbt
        acc_ref[...] = jnp.zeros_like(acc_ref)

        def per_head(hd, carry):
            top_ref[...] = jnp.full(top_ref.shape, NEG_INF, F32)
            rnk_ref[...] = jnp.full(rnk_ref.shape, 2.0 * nsel, F32)
            qoff = pl.multiple_of(hd * 2 * half, 2 * half)
            q_hd = jnp.dot(wqt_ref[pl.ds(qoff, 2 * half), :], hbt_ref[...],
                           preferred_element_type=F32)
            for c, dst in ((0, s0_ref), (1, s1_ref)):
                off = pl.multiple_of((hd * 2 + c) * nk, nk)
                sc = jnp.dot(keys_ref[pl.ds(off, nk), :], q_hd[c * half:(c + 1) * half, :],
                             precision=HIGHEST, preferred_element_type=F32)
                dst[...] = sc
                work_ref[c] = sc

            def take(r, carry):
                for c in range(2):
                    cur = work_ref[c]
                    m = jnp.max(cur, axis=0, keepdims=True)
                    top_ref[c, pl.ds(r, 1), :] = m
                    hit = cur >= m
                    work_ref[c] = jnp.where(hit, NEG_INF, cur)
                    if c == 1:
                        rnk_ref[...] = jnp.where(hit, lax.convert_element_type(r + 1, F32),
                                                 rnk_ref[...])
                return carry

            lax.fori_loop(0, nsel, take, 0)
            v0 = top_ref[0]
            v1 = top_ref[1]
            cands = [v0[0:1, :] + v1]
            cands += [v0[a:a + 1, :] + v1[0:8, :] for a in range(1, 8)]
            cands += [v0[8:pad_rows, :] + v1[0:1, :]]
            cand = jnp.concatenate(cands, axis=0)
            cand_ref[...] = cand

            def take2(r, carry):
                cur = cand_ref[...]
                m = jnp.max(cur, axis=0, keepdims=True)
                ord_ref[pl.ds(r, 1), :] = m
                cand_ref[...] = jnp.where(cur >= m, NEG_INF, cur)
                return carry

            lax.fori_loop(0, nsel, take2, 0)
            tau = 0.5 * (ord_ref[nsel - 2:nsel - 1, :] + ord_ref[nsel - 1:nsel, :])
            top = v0[0:1, :] + v1[0:1, :]
            z = jnp.sum(jnp.where(cand > tau, jnp.exp(cand - top), 0.0), axis=0, keepdims=True)
            s0 = s0_ref[...]
            theta = tau - s0
            count = jnp.zeros_like(theta)
            for bb in range(PEER_TOPK):
                count = count + (v1[bb:bb + 1, :] >= theta).astype(F32)
            n_ref[hd] = count
            r_ref[hd] = jnp.exp(s0 - v0[0:1, :]) / z
            e1 = jnp.exp(s1_ref[...] - v1[0:1, :])
            e1_ref[hd] = e1.astype(BF16).reshape(nk // pk, pk, tm)
            rank_ref[hd] = rnk_ref[...].astype(BF16).reshape(nk // pk, pk, tm)
            return carry

        lax.fori_loop(0, heads, per_head, 0)

    lanes = 128
    keys_per_step = eb // nk
    sub_keys = 4
    base = pl.multiple_of(j * keys_per_step, keys_per_step)
    hbt = hbt_ref[...]
    sub_rows = sub_keys * nk
    n_sub = keys_per_step // sub_keys

    def expert_acts(sub):
        return jnp.dot(u_ref[sub * sub_rows:(sub + 1) * sub_rows, :], hbt,
                       preferred_element_type=F32)

    def project(sub, gate):
        acc_ref[...] += jnp.dot(vt_ref[:, sub * sub_rows:(sub + 1) * sub_rows], gate,
                                preferred_element_type=F32)

    a_next = expert_acts(0)
    gate_prev = None
    for sub in range(n_sub):
        a_sub = a_next
        if sub + 1 < n_sub:
            a_next = expert_acts(sub + 1)
        if gate_prev is not None:
            project(sub - 1, gate_prev)
        rows = []
        for k2 in range(sub_keys):
            ii = sub * sub_keys + k2
            cols = []
            for lt in range(tm // lanes):
                ls = slice(lt * lanes, (lt + 1) * lanes)
                w = jnp.zeros((nk // pk, pk, lanes), BF16)
                for hd in range(heads):
                    cnt = n_ref[hd, pl.ds(base, keys_per_step), ls][ii:ii + 1, :]
                    rr = r_ref[hd, pl.ds(base, keys_per_step), ls][ii:ii + 1, :]
                    cnt = jnp.broadcast_to(cnt, (pk, lanes)).astype(BF16)[None]
                    rr = jnp.broadcast_to(rr, (pk, lanes)).astype(BF16)[None]
                    w = w + jnp.where(rank_ref[hd, :, :, ls] <= cnt, e1_ref[hd, :, :, ls],
                                      jnp.zeros((), BF16)) * rr
                a = a_sub[k2 * nk:(k2 + 1) * nk, ls].astype(BF16).reshape(nk // pk, pk, lanes)
                cols.append((w * jax.nn.gelu(a)).reshape(nk, lanes))
            rows.append(jnp.concatenate(cols, axis=1))
        gate_prev = jnp.concatenate(rows, axis=0)
    project(n_sub - 1, gate_prev)

    @pl.when(j == pl.num_programs(1) - 1)
    def _finish():
        row = _mod_row(i, n_ctx_tiles, lat_tiles, ctx_row)
        m5 = mod_ref[pl.ds(row, 1), 5 * d:6 * d]
        f = acc_ref[...].T
        o_ref[...] = _layer_norm(alpha * x_ref[...] + m5 * f, g_ref[...], b_ref[...])


def _peer(x, mod, wqt, keys2d, u_b, vt_b, g, b, *, tm, heads, alpha,
          skip_tiles, n_ctx_tiles, lat_tiles, ctx_row):
    t, d = x.shape
    nblk, _, eb = vt_b.shape
    nk = keys2d.shape[0] // (2 * heads)
    half = keys2d.shape[1]
    pad_rows = 24
    ncand = pad_rows + 7 * 8 + (pad_rows - 8)
    kern = functools.partial(_peer_kernel, d=d, nk=nk, heads=heads, half=half, alpha=alpha,
                             n_ctx_tiles=n_ctx_tiles, lat_tiles=lat_tiles, ctx_row=ctx_row)
    const = lambda i, j: (0, 0)
    once = dict(pipeline_mode=pl.Buffered(1))
    return pl.pallas_call(
        kern,
        grid=(t // tm - skip_tiles, nblk),
        in_specs=[pl.BlockSpec((tm, d), lambda i, j: (i + skip_tiles, 0), **once),
                  pl.BlockSpec(mod.shape, const, **once),
                  pl.BlockSpec(wqt.shape, const, **once),
                  pl.BlockSpec(keys2d.shape, const, **once),
                  pl.BlockSpec((eb, d), lambda i, j: (j, 0)),
                  pl.BlockSpec((None, d, eb), lambda i, j: (j, 0, 0)),
                  pl.BlockSpec(g.shape, const, **once), pl.BlockSpec(b.shape, const, **once)],
        out_specs=pl.BlockSpec((tm, d), lambda i, j: (i, 0)),
        out_shape=jax.ShapeDtypeStruct((t - skip_tiles * tm, d), F32),
        scratch_shapes=[pltpu.VMEM((d, tm), BF16),
                        pltpu.VMEM((d, tm), F32),
                        pltpu.VMEM((2, pad_rows, tm), F32),
                        pltpu.VMEM((ncand, tm), F32),
                        pltpu.VMEM((pad_rows, tm), F32),
                        pltpu.VMEM((2, nk, tm), F32),
                        pltpu.VMEM((nk, tm), F32),
                        pltpu.VMEM((nk, tm), F32),
                        pltpu.VMEM((nk, tm), F32),
                        pltpu.VMEM((heads, nk // 16, 16, tm), BF16),
                        pltpu.VMEM((heads, nk // 16, 16, tm), BF16),
                        pltpu.VMEM((heads, nk, tm), F32),
                        pltpu.VMEM((heads, nk, tm), F32)],
        compiler_params=_cparams(("arbitrary", "arbitrary")),
        name="peer_ln",
    )(x, mod, wqt, keys2d, u_b, vt_b, g, b)


def _pick_tile(limit, *sizes):
    tm = limit
    while any(s % tm for s in sizes):
        tm //= 2
    return tm


def kernel(x, c, ctx, c_ctx, w_mod, b_mod, w_in, conv_w, conv_b, ssd_a_log, ssd_dt_bias, ssd_d, ssd_norm_g, s5_lam_re, s5_lam_im, s5_log_step, s5_b_re, s5_b_im, s5_c_re, s5_c_im, s5_d, s5_glu_w, s5_glu_b, w_out, ln1_g, ln1_b, peer_wq, peer_keys, peer_u, peer_v, ln2_g, ln2_b):
    nb, seq, d = x.shape
    ctx_len = ctx.shape[1]
    depth = w_mod.shape[0]
    s5_width = s5_d.shape[1]
    heads = peer_keys.shape[1]
    nk = peer_keys.shape[3]
    alpha = (2.0 * depth) ** 0.25
    rows_lat = seq // GRID_W

    tm = _pick_tile(256, nb * ctx_len, seq)
    tp = _pick_tile(1024, nb * ctx_len, seq)
    q = _pick_tile(128, ctx_len, seq)
    steps = _pick_tile(64, ctx_len, seq)
    eb = 8 * nk
    ctx_row = nb

    mod_rows = -(-(nb + 1) // 8) * 8
    cc = jnp.concatenate([c, c_ctx[None, :], jnp.zeros((mod_rows - nb - 1, d), F32)], axis=0)
    mods = _mod_table(cc, w_mod, b_mod)

    xt = jnp.concatenate([ctx.reshape(nb * ctx_len, d), x.reshape(nb * seq, d)], axis=0)

    def pad_lanes(w, n):
        return jnp.pad(w, ((0, 0), (0, n - w.shape[1])))

    for i in range(depth):
        wi = w_in[i]
        o_dt = SSD_WIDTH + XBC_WIDTH
        w_pad = jnp.concatenate(
            [wi[:, :o_dt],
             pad_lanes(wi[:, o_dt:o_dt + SSD_HEADS], HEAD_LANES),
             pad_lanes(wi[:, o_dt + SSD_HEADS:o_dt + 2 * SSD_HEADS], HEAD_LANES),
             wi[:, o_dt + 2 * SSD_HEADS:]], axis=1).astype(BF16)
        a_rows = pad_lanes(-jnp.exp(ssd_a_log[i]), HEAD_LANES)
        dtb_rows = pad_lanes(ssd_dt_bias[i], HEAD_LANES)
        dskip_ssd = jnp.repeat(ssd_d[i], SSD_HEAD_DIM)[None, :]
        bfull, abar, cfull = _s5_params(s5_lam_re[i], s5_lam_im[i], s5_log_step[i],
                                        s5_b_re[i], s5_b_im[i], s5_c_re[i], s5_c_im[i], nb)

        z, xbc, dt, u = _in_proj(xt, mods[i], w_pad, tm=tm, n_ctx_tiles=nb * ctx_len // tm,
                                 lat_tiles=seq // tm, ctx_row=ctx_row, s5_width=s5_width)
        ssd_args = dict(nb=nb, ctx_len=ctx_len, seq=seq, q=q)
        y_f = _ssd_pass(xbc, dt, None, None, conv_w[i], conv_b[i][None, :], a_rows[0:1],
                        dtb_rows[0:1], None, None, direction=0, **ssd_args)
        ssd_out = _ssd_pass(xbc, dt, z, y_f, conv_w[i], conv_b[i][None, :], a_rows[1:2],
                            dtb_rows[1:2], dskip_ssd, ssd_norm_g[i][None, :], direction=1,
                            **ssd_args)

        u_ctx = u[:nb * ctx_len].reshape(nb, ctx_len, s5_width).transpose(1, 0, 2)
        u_lat = u[nb * ctx_len:].reshape(nb, rows_lat, GRID_W, s5_width).transpose(2, 1, 0, 3)
        u_scan = jnp.concatenate([u_ctx.reshape(ctx_len * nb, s5_width),
                                  u_lat.reshape(seq * nb, s5_width)], axis=0)
        s5_args = dict(nb=nb, ctx_len=ctx_len, steps=steps)
        ys_f = _s5_pass(u_scan, None, bfull[0], abar[0], cfull, None, None, None,
                        direction=0, **s5_args)
        s5_scan = _s5_pass(u_scan, ys_f, bfull[1], abar[1], cfull, s5_d[i][None, :],
                           s5_glu_w[i].astype(BF16), s5_glu_b[i][None, :], direction=1,
                           **s5_args)
        s5_ctx = s5_scan[:ctx_len * nb].reshape(ctx_len, nb, s5_width).transpose(1, 0, 2)
        s5_lat = s5_scan[ctx_len * nb:].reshape(GRID_W, rows_lat, nb, s5_width)
        s5_lat = s5_lat.transpose(2, 1, 0, 3)
        s5_out = jnp.concatenate([s5_ctx.reshape(nb * ctx_len, s5_width),
                                  s5_lat.reshape(nb * seq, s5_width)], axis=0)

        wo = w_out[i].astype(BF16)
        xt = _out_proj(xt, ssd_out, s5_out, mods[i], wo[:SSD_WIDTH], wo[SSD_WIDTH:],
                       ln1_g[i][None, :], ln1_b[i][None, :], tm=tm, alpha=alpha,
                       n_ctx_tiles=nb * ctx_len // tm, lat_tiles=seq // tm, ctx_row=ctx_row)

        last = i == depth - 1
        ctx_tiles = nb * ctx_len // tp
        vt_blocks = peer_v[i].astype(BF16).reshape(nk * nk // eb, eb, d).transpose(0, 2, 1)
        xt = _peer(xt, mods[i], peer_wq[i].T.astype(BF16),
                   peer_keys[i].reshape(heads * 2 * nk, -1),
                   peer_u[i].astype(BF16), vt_blocks,
                   ln2_g[i][None, :], ln2_b[i][None, :], tm=tp, heads=heads, alpha=alpha,
                   skip_tiles=ctx_tiles if last else 0,
                   n_ctx_tiles=0 if last else ctx_tiles, lat_tiles=seq // tp, ctx_row=ctx_row)

    return xt.reshape(nb, seq, d)
```

```python
import functools
import math

import jax
import jax.numpy as jnp
from jax import lax
from jax.experimental import pallas as pl
from jax.experimental.pallas import tpu as pltpu

GRID_W = 64

SSD_HEADS = 12
SSD_HEAD_DIM = 64
SSD_GROUPS = 2
SSD_STATE = 128
SSD_WIDTH = SSD_HEADS * SSD_HEAD_DIM
XBC_WIDTH = SSD_WIDTH + 2 * SSD_GROUPS * SSD_STATE
HEAD_LANES = 128
S5_GROUP = 16
S5_STATE = 64
PEER_TOPK = 16
EPS = 1e-5

F32 = jnp.float32
BF16 = jnp.bfloat16
HIGHEST = lax.Precision.HIGHEST
NEG_INF = float("-inf")

VMEM_LIMIT = 56 * 1024 * 1024


def _cparams(sem):
    return pltpu.CompilerParams(dimension_semantics=sem, vmem_limit_bytes=VMEM_LIMIT)


def _silu(x):
    return x * jax.nn.sigmoid(x)


def _softplus(x):
    return jnp.maximum(x, 0.0) + jnp.log1p(jnp.exp(-jnp.abs(x)))


def _dot_select(x, sel):
    hi = x.astype(BF16)
    r1 = x - hi.astype(F32)
    mid = r1.astype(BF16)
    lo = (r1 - mid.astype(F32)).astype(BF16)
    return (jnp.dot(hi, sel, preferred_element_type=F32)
            + jnp.dot(mid, sel, preferred_element_type=F32)
            + jnp.dot(lo, sel, preferred_element_type=F32))


def _layer_norm(y, g, b):
    mu = jnp.mean(y, axis=-1, keepdims=True)
    d = y - mu
    var = jnp.mean(d * d, axis=-1, keepdims=True)
    return d * lax.rsqrt(var + EPS) * g + b


def _mod_row(tile, n_ctx_tiles, lat_tiles_per_batch, ctx_row):
    return jnp.where(tile < n_ctx_tiles, ctx_row,
                     (tile - n_ctx_tiles) // lat_tiles_per_batch)


def _mod_kernel(c_ref, w_ref, b_ref, o_ref):
    o_ref[...] = jnp.dot(_silu(c_ref[...]), w_ref[...], precision=HIGHEST,
                         preferred_element_type=F32) + b_ref[...]


def _mod_table(cc, w_mod, b_mod):
    depth, d, d6 = w_mod.shape
    rows = cc.shape[0]
    nj = d6 // d
    return pl.pallas_call(
        _mod_kernel,
        grid=(depth, nj),
        in_specs=[pl.BlockSpec((rows, d), lambda l, j: (0, 0)),
                  pl.BlockSpec((None, d, d), lambda l, j: (l, 0, j)),
                  pl.BlockSpec((None, 1, d), lambda l, j: (l, 0, j))],
        out_specs=pl.BlockSpec((None, rows, d), lambda l, j: (l, 0, j)),
        out_shape=jax.ShapeDtypeStruct((depth, rows, d6), F32),
        compiler_params=_cparams(("arbitrary", "arbitrary")),
        name="mod_table",
    )(cc, w_mod, b_mod.reshape(depth, 1, d6))


def _in_proj_kernel(x_ref, mod_ref, w_ref, z_ref, xbc_ref, dt_ref, u_ref, *,
                    d, n_ctx_tiles, lat_tiles, ctx_row):
    row = _mod_row(pl.program_id(0), n_ctx_tiles, lat_tiles, ctx_row)
    m0 = mod_ref[pl.ds(row, 1), 0:d]
    m1 = mod_ref[pl.ds(row, 1), d:2 * d]
    xm = (x_ref[...] * (1.0 + m1) + m0).astype(BF16)
    p = jnp.dot(xm, w_ref[...], preferred_element_type=F32)
    o = 0
    for ref in (z_ref, xbc_ref, dt_ref, u_ref):
        w = ref.shape[-1]
        ref[...] = p[:, o:o + w]
        o += w


def _in_proj(x, mod, w, *, tm, n_ctx_tiles, lat_tiles, ctx_row, s5_width):
    t, d = x.shape
    widths = (SSD_WIDTH, XBC_WIDTH, 2 * HEAD_LANES, s5_width)
    kern = functools.partial(_in_proj_kernel, d=d, n_ctx_tiles=n_ctx_tiles,
                             lat_tiles=lat_tiles, ctx_row=ctx_row)
    return pl.pallas_call(
        kern,
        grid=(t // tm,),
        in_specs=[pl.BlockSpec((tm, d), lambda i: (i, 0)),
                  pl.BlockSpec(mod.shape, lambda i: (0, 0)),
                  pl.BlockSpec(w.shape, lambda i: (0, 0))],
        out_specs=[pl.BlockSpec((tm, wd), lambda i: (i, 0)) for wd in widths],
        out_shape=[jax.ShapeDtypeStruct((t, wd), F32) for wd in widths],
        compiler_params=_cparams(("arbitrary",)),
        name="in_proj",
    )(x, mod, w)


def _ssd_chunk_index(s, *, nctx, nlat, reverse):
    if not reverse:
        return s
    return jnp.where(s < nctx, nctx - 1 - s, nctx + nlat - 1 - (s - nctx))


def _ssd_row_block(b, c, *, nb, nctx, nlat):
    return jnp.where(c < nctx, b * nctx + c, nb * nctx + b * nlat + (c - nctx))


def _ssd_kernel(*refs, q, nctx, nlat, reverse, final):
    if final:
        (xbc_ref, prev_ref, next_ref, dt_ref, z_ref, yprev_ref, convw_ref, convb_ref,
         a_ref, dtb_ref, dskip_ref, ng_ref, out_ref, h_ref) = refs
    else:
        (xbc_ref, prev_ref, next_ref, dt_ref, convw_ref, convb_ref,
         a_ref, dtb_ref, out_ref, h_ref) = refs
    s = pl.program_id(1)
    c = _ssd_chunk_index(s, nctx=nctx, nlat=nlat, reverse=reverse)

    @pl.when(s == 0)
    def _():
        h_ref[...] = jnp.zeros_like(h_ref)

    x = xbc_ref[...]
    rows = lax.broadcasted_iota(jnp.int32, x.shape, 0)
    has_prev = jnp.logical_and(c != 0, c != nctx).astype(F32)
    has_next = jnp.logical_and(c != nctx - 1, c != nctx + nlat - 1).astype(F32)
    xm1 = jnp.where(rows == 0, prev_ref[7:8, :] * has_prev, pltpu.roll(x, 1, 0))
    xp1 = jnp.where(rows == q - 1, next_ref[0:1, :] * has_next, pltpu.roll(x, q - 1, 0))
    conv = (convw_ref[0:1, :] * xm1 + convw_ref[1:2, :] * x + convw_ref[2:3, :] * xp1
            + convb_ref[...])
    act = _silu(conv)
    xs = act[:, :SSD_WIDTH]
    bm = act[:, SSD_WIDTH:SSD_WIDTH + SSD_GROUPS * SSD_STATE]
    cm = act[:, SSD_WIDTH + SSD_GROUPS * SSD_STATE:]

    dt = _softplus(dt_ref[...] + dtb_ref[...])
    a = dt * a_ref[...]
    li = lax.broadcasted_iota(jnp.int32, (q, q), 0)
    si = lax.broadcasted_iota(jnp.int32, (q, q), 1)
    mask = (si >= li) if reverse else (si <= li)
    mask_t = (li >= si) if reverse else (li <= si)
    cum_t = _dot_select(a.T, mask_t.astype(BF16))
    cum = cum_t.T
    edge = 0 if reverse else q - 1

    hh = lax.broadcasted_iota(jnp.int32, (HEAD_LANES, SSD_WIDTH), 0)
    ll = lax.broadcasted_iota(jnp.int32, (HEAD_LANES, SSD_WIDTH), 1)
    expand = (lax.shift_right_logical(ll, int(math.log2(SSD_HEAD_DIM))) == hh).astype(BF16)
    dt_e = _dot_select(dt, expand)
    cum_e = _dot_select(cum, expand)
    tot_e = cum_e[edge:edge + 1, :]
    xdt = xs * dt_e
    xdec = (xdt * jnp.exp(tot_e - cum_e)).astype(BF16)
    xdt_b = xdt.astype(BF16)
    off_scale = jnp.exp(cum_e)

    gw = SSD_WIDTH // SSD_GROUPS
    heads_per_group = SSD_HEADS // SSD_GROUPS
    lane = lax.broadcasted_iota(jnp.int32, (q, 2 * SSD_HEAD_DIM), 1)
    h_old = h_ref[...]
    y_groups = []
    st_groups = []
    for g in range(SSD_GROUPS):
        bg = bm[:, g * SSD_STATE:(g + 1) * SSD_STATE]
        cg = cm[:, g * SSD_STATE:(g + 1) * SSD_STATE].astype(BF16)
        cb = lax.dot_general(cg, bg.astype(BF16), (((1,), (1,)), ((), ())),
                             preferred_element_type=F32)
        pairs = []
        for pr in range(heads_per_group // 2):
            lo = g * gw + pr * 2 * SSD_HEAD_DIM
            xp = xdt_b[:, lo:lo + 2 * SSD_HEAD_DIM]
            ys = []
            for k in range(2):
                h = g * heads_per_group + pr * 2 + k
                seg = cum[:, h:h + 1] - cum_t[h:h + 1, :]
                lm = jnp.exp(jnp.where(mask, seg, NEG_INF))
                ys.append(jnp.dot((cb * lm).astype(BF16), xp, preferred_element_type=F32))
            pairs.append(jnp.where(lane < SSD_HEAD_DIM, ys[0], ys[1]))
        y_diag = jnp.concatenate(pairs, axis=1)
        hg = h_old[:, g * gw:(g + 1) * gw]
        y_off = jnp.dot(cg, hg.astype(BF16), preferred_element_type=F32)
        y_groups.append(y_diag + y_off * off_scale[:, g * gw:(g + 1) * gw])
        st_groups.append(jnp.dot(bg.T.astype(BF16), xdec[:, g * gw:(g + 1) * gw],
                                 preferred_element_type=F32))
    y = jnp.concatenate(y_groups, axis=1)
    h_ref[...] = h_old * jnp.exp(tot_e) + jnp.concatenate(st_groups, axis=1)

    if final:
        y = y + yprev_ref[...] + xs * dskip_ref[...]
        gated = y * _silu(z_ref[...])
        ms = jnp.mean(gated * gated, axis=-1, keepdims=True)
        out_ref[...] = gated * lax.rsqrt(ms + EPS) * ng_ref[...]
    else:
        out_ref[...] = y


def _ssd_pass(xbc, dt, z, yprev, convw, convb, a_row, dtb_row, dskip, ng, *,
              nb, ctx_len, seq, q, direction):
    t = xbc.shape[0]
    nctx, nlat = ctx_len // q, seq // q
    reverse = direction == 1
    final = direction == 1
    q8 = q // 8
    last8 = t // 8 - 1

    def blk(b, s):
        c = _ssd_chunk_index(s, nctx=nctx, nlat=nlat, reverse=reverse)
        return _ssd_row_block(b, c, nb=nb, nctx=nctx, nlat=nlat)

    row_map = lambda b, s: (blk(b, s), 0)
    prev_map = lambda b, s: (jnp.maximum(blk(b, s) * q8 - 1, 0), 0)
    next_map = lambda b, s: (jnp.minimum((blk(b, s) + 1) * q8, last8), 0)
    dt_map = lambda b, s: (blk(b, s), direction)
    const = lambda b, s: (0, 0)

    in_specs = [pl.BlockSpec((q, XBC_WIDTH), row_map),
                pl.BlockSpec((8, XBC_WIDTH), prev_map),
                pl.BlockSpec((8, XBC_WIDTH), next_map),
                pl.BlockSpec((q, HEAD_LANES), dt_map)]
    args = [xbc, xbc, xbc, dt]
    if final:
        in_specs += [pl.BlockSpec((q, SSD_WIDTH), row_map)] * 2
        args += [z, yprev]
    in_specs += [pl.BlockSpec(convw.shape, const), pl.BlockSpec(convb.shape, const),
                 pl.BlockSpec(a_row.shape, const), pl.BlockSpec(dtb_row.shape, const)]
    args += [convw, convb, a_row, dtb_row]
    if final:
        in_specs += [pl.BlockSpec(dskip.shape, const), pl.BlockSpec(ng.shape, const)]
        args += [dskip, ng]
    kern = functools.partial(_ssd_kernel, q=q, nctx=nctx, nlat=nlat,
                             reverse=reverse, final=final)
    return pl.pallas_call(
        kern,
        grid=(nb, nctx + nlat),
        in_specs=in_specs,
        out_specs=pl.BlockSpec((q, SSD_WIDTH), row_map),
        out_shape=jax.ShapeDtypeStruct((t, SSD_WIDTH), F32),
        scratch_shapes=[pltpu.VMEM((SSD_STATE, SSD_WIDTH), F32)],
        compiler_params=_cparams(("arbitrary", "arbitrary")),
        name="ssd_bwd" if final else "ssd_fwd",
    )(*args)


def _s5_block_index(i, *, nctx, nblk, reverse):
    if not reverse:
        return i
    return jnp.where(i < nctx, nctx - 1 - i, nblk - 1 - (i - nctx))


def _s5_kernel(*refs, steps, nb, reverse, final):
    if final:
        (u_ref, yprev_ref, bfull_ref, abar_ref, cfull_ref, dskip_ref, gluw_ref, glub_ref,
         out_ref, state_ref, x_ref) = refs
    else:
        u_ref, bfull_ref, abar_ref, cfull_ref, out_ref, state_ref, x_ref = refs

    @pl.when(pl.program_id(0) == 0)
    def _():
        state_ref[...] = jnp.zeros_like(state_ref)

    u = u_ref[...]
    x_ref[...] = jnp.dot(u.astype(BF16), bfull_ref[...], preferred_element_type=F32)
    n = abar_ref.shape[1] // 2
    half = n // 2
    for part in range(2):
        re = slice(part * half, (part + 1) * half)
        im = slice(n + part * half, n + (part + 1) * half)
        ar = abar_ref[:, re]
        ai = abar_ref[:, im]

        def step(k, carry, re=re, im=im, ar=ar, ai=ai):
            xr, xi = carry
            kk = steps - 1 - k if reverse else k
            r0 = pl.multiple_of(kk * nb, nb)
            nr = ar * xr - ai * xi + x_ref[pl.ds(r0, nb), re]
            ni = ar * xi + ai * xr + x_ref[pl.ds(r0, nb), im]
            x_ref[pl.ds(r0, nb), re] = nr
            x_ref[pl.ds(r0, nb), im] = ni
            return nr, ni

        xr, xi = lax.fori_loop(0, steps, step, (state_ref[:, re], state_ref[:, im]),
                               unroll=4)
        state_ref[:, re] = xr
        state_ref[:, im] = xi

    y = jnp.dot(x_ref[...].astype(BF16), cfull_ref[...], preferred_element_type=F32)
    if final:
        y = jax.nn.gelu(y + yprev_ref[...] + u * dskip_ref[...])
        gate = jnp.dot(y.astype(BF16), gluw_ref[...], preferred_element_type=F32)
        out_ref[...] = y * jax.nn.sigmoid(gate + glub_ref[...])
    else:
        out_ref[...] = y


def _s5_pass(u_scan, yprev, bfull, abar, cfull, dskip, gluw, glub, *,
             nb, ctx_len, steps, direction):
    rows, width = u_scan.shape
    nblk = rows // (steps * nb)
    nctx = ctx_len // steps
    reverse = direction == 1
    final = direction == 1
    row_map = lambda i: (_s5_block_index(i, nctx=nctx, nblk=nblk, reverse=reverse), 0)
    const = lambda i: (0, 0)
    blk = pl.BlockSpec((steps * nb, width), row_map)
    in_specs = [blk]
    args = [u_scan]
    if final:
        in_specs.append(blk)
        args.append(yprev)
    in_specs += [pl.BlockSpec(bfull.shape, const), pl.BlockSpec(abar.shape, const),
                 pl.BlockSpec(cfull.shape, const)]
    args += [bfull, abar, cfull]
    if final:
        in_specs += [pl.BlockSpec(dskip.shape, const), pl.BlockSpec(gluw.shape, const),
                     pl.BlockSpec(glub.shape, const)]
        args += [dskip, gluw, glub]
    kern = functools.partial(_s5_kernel, steps=steps, nb=nb, reverse=reverse, final=final)
    return pl.pallas_call(
        kern,
        grid=(nblk,),
        in_specs=in_specs,
        out_specs=blk,
        out_shape=jax.ShapeDtypeStruct((rows, width), F32),
        scratch_shapes=[pltpu.VMEM(abar.shape, F32),
                        pltpu.VMEM((steps * nb, abar.shape[1]), F32)],
        compiler_params=_cparams(("arbitrary",)),
        name="s5_bwd" if final else "s5_fwd",
    )(*args)


def _s5_params(lam_re, lam_im, log_step, b_re, b_im, c_re, c_im, nb):
    g, p = lam_re.shape[1], lam_re.shape[2]
    step = jnp.exp(log_step)[..., None]
    mag = jnp.exp(lam_re * step)
    ar = mag * jnp.cos(lam_im * step)
    ai = mag * jnp.sin(lam_im * step)
    den = lam_re * lam_re + lam_im * lam_im
    kr = ((ar - 1.0) * lam_re + ai * lam_im) / den
    ki = (ai * lam_re - (ar - 1.0) * lam_im) / den
    bbr = kr[..., None] * b_re - ki[..., None] * b_im
    bbi = kr[..., None] * b_im + ki[..., None] * b_re
    eye = jnp.eye(g, dtype=F32)
    cdim = b_re.shape[-1]

    def bmat(bb):
        return jnp.einsum('gpc,gh->gchp', bb, eye).reshape(g * cdim, g * p)

    bfull = jnp.stack([jnp.concatenate([bmat(bbr[d]), bmat(bbi[d])], axis=1)
                       for d in range(2)]).astype(BF16)
    abar = jnp.stack([jnp.broadcast_to(
        jnp.concatenate([ar[d].reshape(-1), ai[d].reshape(-1)])[None, :], (nb, 2 * g * p))
        for d in range(2)])

    def cmat(cc):
        return jnp.einsum('gcp,gh->gphc', cc, eye).reshape(g * p, g * cdim)

    cfull = jnp.concatenate([cmat(c_re), -cmat(c_im)], axis=0).astype(BF16)
    return bfull, abar, cfull


def _out_proj_kernel(x_ref, ssd_ref, s5_ref, mod_ref, w1_ref, w2_ref, g_ref, b_ref, o_ref, *,
                     d, alpha, n_ctx_tiles, lat_tiles, ctx_row):
    row = _mod_row(pl.program_id(0), n_ctx_tiles, lat_tiles, ctx_row)
    m2 = mod_ref[pl.ds(row, 1), 2 * d:3 * d]
    mix = (jnp.dot(ssd_ref[...].astype(BF16), w1_ref[...], preferred_element_type=F32)
           + jnp.dot(s5_ref[...].astype(BF16), w2_ref[...], preferred_element_type=F32))
    o_ref[...] = _layer_norm(alpha * x_ref[...] + m2 * mix, g_ref[...], b_ref[...])


def _out_proj(x, ssd, s5, mod, w1, w2, g, b, *, tm, alpha, n_ctx_tiles, lat_tiles, ctx_row):
    t, d = x.shape
    kern = functools.partial(_out_proj_kernel, d=d, alpha=alpha, n_ctx_tiles=n_ctx_tiles,
                             lat_tiles=lat_tiles, ctx_row=ctx_row)
    const = lambda i: (0, 0)
    return pl.pallas_call(
        kern,
        grid=(t // tm,),
        in_specs=[pl.BlockSpec((tm, d), lambda i: (i, 0)),
                  pl.BlockSpec((tm, ssd.shape[1]), lambda i: (i, 0)),
                  pl.BlockSpec((tm, s5.shape[1]), lambda i: (i, 0)),
                  pl.BlockSpec(mod.shape, const), pl.BlockSpec(w1.shape, const),
                  pl.BlockSpec(w2.shape, const), pl.BlockSpec(g.shape, const),
                  pl.BlockSpec(b.shape, const)],
        out_specs=pl.BlockSpec((tm, d), lambda i: (i, 0)),
        out_shape=jax.ShapeDtypeStruct((t, d), F32),
        compiler_params=_cparams(("arbitrary",)),
        name="out_proj_ln",
    )(x, ssd, s5, mod, w1, w2, g, b)


def _peer_kernel(x_ref, mod_ref, wqt_ref, keys_ref, u_ref, vt_ref, g_ref, b_ref, o_ref,
                 hbt_ref, acc_ref, top_ref, cand_ref, ord_ref, work_ref, s0_ref, s1_ref,
                 rnk_ref, rank_ref, e1_ref, n_ref, r_ref, *,
                 d, nk, heads, half, alpha, n_ctx_tiles, lat_tiles, ctx_row):
    i = pl.program_id(0)
    j = pl.program_id(1)
    tm = x_ref.shape[0]
    eb = u_ref.shape[0]
    nsel = PEER_TOPK + 1
    pad_rows = top_ref.shape[1]
    pk = 16

    @pl.when(j == 0)
    def _prepare():
        row = _mod_row(i, n_ctx_tiles, lat_tiles, ctx_row)
        m3 = mod_ref[pl.ds(row, 1), 3 * d:4 * d]
        m4 = mod_ref[pl.ds(row, 1), 4 * d:5 * d]
        h = x_ref[...] * (1.0 + m4) + m3
        hbt_ref[...] = h.T.astype(BF16)
        acc_ref[...] = jnp.zeros_like(acc_ref)

        def per_head(hd, carry):
            top_ref[...] = jnp.full(top_ref.shape, NEG_INF, F32)
            rnk_ref[...] = jnp.full(rnk_ref.shape, 2.0 * nsel, F32)
            qoff = pl.multiple_of(hd * 2 * half, 2 * half)
            q_hd = jnp.dot(wqt_ref[pl.ds(qoff, 2 * half), :], hbt_ref[...],
                           preferred_element_type=F32)
            for c, dst in ((0, s0_ref), (1, s1_ref)):
                off = pl.multiple_of((hd * 2 + c) * nk, nk)
                sc = jnp.dot(keys_ref[pl.ds(off, nk), :], q_hd[c * half:(c + 1) * half, :],
                             precision=HIGHEST, preferred_element_type=F32)
                dst[...] = sc
                work_ref[c] = sc

            def take(r, carry):
                for c in range(2):
                    cur = work_ref[c]
                    m = jnp.max(cur, axis=0, keepdims=True)
                    top_ref[c, pl.ds(r, 1), :] = m
                    hit = cur >= m
                    work_ref[c] = jnp.where(hit, NEG_INF, cur)
                    if c == 1:
                        rnk_ref[...] = jnp.where(hit, lax.convert_element_type(r + 1, F32),
                                                 rnk_ref[...])
                return carry

            lax.fori_loop(0, nsel, take, 0)
            v0 = top_ref[0]
            v1 = top_ref[1]
            cands = [v0[0:1, :] + v1]
            cands += [v0[a:a + 1, :] + v1[0:8, :] for a in range(1, 8)]
            cands += [v0[8:pad_rows, :] + v1[0:1, :]]
            cand = jnp.concatenate(cands, axis=0)
            cand_ref[...] = cand

            def take2(r, carry):
                cur = cand_ref[...]
                m = jnp.max(cur, axis=0, keepdims=True)
                ord_ref[pl.ds(r, 1), :] = m
                cand_ref[...] = jnp.where(cur >= m, NEG_INF, cur)
                return carry

            lax.fori_loop(0, nsel, take2, 0)
            tau = 0.5 * (ord_ref[nsel - 2:nsel - 1, :] + ord_ref[nsel - 1:nsel, :])
            top = v0[0:1, :] + v1[0:1, :]
            z = jnp.sum(jnp.where(cand > tau, jnp.exp(cand - top), 0.0), axis=0, keepdims=True)
            s0 = s0_ref[...]
            theta = tau - s0
            count = jnp.zeros_like(theta)
            for bb in range(PEER_TOPK):
                count = count + (v1[bb:bb + 1, :] >= theta).astype(F32)
            n_ref[hd] = count
            r_ref[hd] = jnp.exp(s0 - v0[0:1, :]) / z
            e1 = jnp.exp(s1_ref[...] - v1[0:1, :])
            e1_ref[hd] = e1.astype(BF16).reshape(nk // pk, pk, tm)
            rank_ref[hd] = rnk_ref[...].astype(BF16).reshape(nk // pk, pk, tm)
            return carry

        lax.fori_loop(0, heads, per_head, 0)

    lanes = 128
    keys_per_step = eb // nk
    sub_keys = 4
    base = pl.multiple_of(j * keys_per_step, keys_per_step)
    hbt = hbt_ref[...]
    sub_rows = sub_keys * nk
    n_sub = keys_per_step // sub_keys

    def expert_acts(sub):
        return jnp.dot(u_ref[sub * sub_rows:(sub + 1) * sub_rows, :], hbt,
                       preferred_element_type=F32)

    def project(sub, gate):
        acc_ref[...] += jnp.dot(vt_ref[:, sub * sub_rows:(sub + 1) * sub_rows], gate,
                                preferred_element_type=F32)

    a_next = expert_acts(0)
    gate_prev = None
    for sub in range(n_sub):
        a_sub = a_next
        if sub + 1 < n_sub:
            a_next = expert_acts(sub + 1)
        if gate_prev is not None:
            project(sub - 1, gate_prev)
        rows = []
        for k2 in range(sub_keys):
            ii = sub * sub_keys + k2
            cols = []
            for lt in range(tm // lanes):
                ls = slice(lt * lanes, (lt + 1) * lanes)
                w = jnp.zeros((nk // pk, pk, lanes), BF16)
                for hd in range(heads):
                    cnt = n_ref[hd, pl.ds(base, keys_per_step), ls][ii:ii + 1, :]
                    rr = r_ref[hd, pl.ds(base, keys_per_step), ls][ii:ii + 1, :]
                    cnt = jnp.broadcast_to(cnt, (pk, lanes)).astype(BF16)[None]
                    rr = jnp.broadcast_to(rr, (pk, lanes)).astype(BF16)[None]
                    w = w + jnp.where(rank_ref[hd, :, :, ls] <= cnt, e1_ref[hd, :, :, ls],
                                      jnp.zeros((), BF16)) * rr
                a = a_sub[k2 * nk:(k2 + 1) * nk, ls].astype(BF16).reshape(nk // pk, pk, lanes)
                cols.append((w * jax.nn.gelu(a)).reshape(nk, lanes))
            rows.append(jnp.concatenate(cols, axis=1))
        gate_prev = jnp.concatenate(rows, axis=0)
    project(n_sub - 1, gate_prev)

    @pl.when(j == pl.num_programs(1) - 1)
    def _finish():
        row = _mod_row(i, n_ctx_tiles, lat_tiles, ctx_row)
        m5 = mod_ref[pl.ds(row, 1), 5 * d:6 * d]
        f = acc_ref[...].T
        o_ref[...] = _layer_norm(alpha * x_ref[...] + m5 * f, g_ref[...], b_ref[...])


def _peer(x, mod, wqt, keys2d, u_b, vt_b, g, b, *, tm, heads, alpha,
          skip_tiles, n_ctx_tiles, lat_tiles, ctx_row):
    t, d = x.shape
    nblk, _, eb = vt_b.shape
    nk = keys2d.shape[0] // (2 * heads)
    half = keys2d.shape[1]
    pad_rows = 24
    ncand = pad_rows + 7 * 8 + (pad_rows - 8)
    kern = functools.partial(_peer_kernel, d=d, nk=nk, heads=heads, half=half, alpha=alpha,
                             n_ctx_tiles=n_ctx_tiles, lat_tiles=lat_tiles, ctx_row=ctx_row)
    const = lambda i, j: (0, 0)
    return pl.pallas_call(
        kern,
        grid=(t // tm - skip_tiles, nblk),
        in_specs=[pl.BlockSpec((tm, d), lambda i, j: (i + skip_tiles, 0)),
                  pl.BlockSpec(mod.shape, const), pl.BlockSpec(wqt.shape, const),
                  pl.BlockSpec(keys2d.shape, const),
                  pl.BlockSpec((eb, d), lambda i, j: (j, 0)),
                  pl.BlockSpec((None, d, eb), lambda i, j: (j, 0, 0)),
                  pl.BlockSpec(g.shape, const), pl.BlockSpec(b.shape, const)],
        out_specs=pl.BlockSpec((tm, d), lambda i, j: (i, 0)),
        out_shape=jax.ShapeDtypeStruct((t - skip_tiles * tm, d), F32),
        scratch_shapes=[pltpu.VMEM((d, tm), BF16),
                        pltpu.VMEM((d, tm), F32),
                        pltpu.VMEM((2, pad_rows, tm), F32),
                        pltpu.VMEM((ncand, tm), F32),
                        pltpu.VMEM((pad_rows, tm), F32),
                        pltpu.VMEM((2, nk, tm), F32),
                        pltpu.VMEM((nk, tm), F32),
                        pltpu.VMEM((nk, tm), F32),
                        pltpu.VMEM((nk, tm), F32),
                        pltpu.VMEM((heads, nk // 16, 16, tm), BF16),
                        pltpu.VMEM((heads, nk // 16, 16, tm), BF16),
                        pltpu.VMEM((heads, nk, tm), F32),
                        pltpu.VMEM((heads, nk, tm), F32)],
        compiler_params=_cparams(("arbitrary", "arbitrary")),
        name="peer_ln",
    )(x, mod, wqt, keys2d, u_b, vt_b, g, b)


def _pick_tile(limit, *sizes):
    tm = limit
    while any(s % tm for s in sizes):
        tm //= 2
    return tm


def kernel(x, c, ctx, c_ctx, w_mod, b_mod, w_in, conv_w, conv_b, ssd_a_log, ssd_dt_bias, ssd_d, ssd_norm_g, s5_lam_re, s5_lam_im, s5_log_step, s5_b_re, s5_b_im, s5_c_re, s5_c_im, s5_d, s5_glu_w, s5_glu_b, w_out, ln1_g, ln1_b, peer_wq, peer_keys, peer_u, peer_v, ln2_g, ln2_b):
    nb, seq, d = x.shape
    ctx_len = ctx.shape[1]
    depth = w_mod.shape[0]
    s5_width = s5_d.shape[1]
    heads = peer_keys.shape[1]
    nk = peer_keys.shape[3]
    alpha = (2.0 * depth) ** 0.25
    rows_lat = seq // GRID_W

    tm = _pick_tile(512, nb * ctx_len, seq)
    tp = _pick_tile(512, nb * ctx_len, seq)
    q = _pick_tile(128, ctx_len, seq)
    steps = _pick_tile(128, ctx_len, seq)
    eb = 16 * nk
    ctx_row = nb

    mod_rows = -(-(nb + 1) // 8) * 8
    cc = jnp.concatenate([c, c_ctx[None, :], jnp.zeros((mod_rows - nb - 1, d), F32)], axis=0)
    mods = _mod_table(cc, w_mod, b_mod)

    xt = jnp.concatenate([ctx.reshape(nb * ctx_len, d), x.reshape(nb * seq, d)], axis=0)

    def pad_lanes(w, n):
        return jnp.pad(w, ((0, 0), (0, n - w.shape[1])))

    for i in range(depth):
        wi = w_in[i]
        o_dt = SSD_WIDTH + XBC_WIDTH
        w_pad = jnp.concatenate(
            [wi[:, :o_dt],
             pad_lanes(wi[:, o_dt:o_dt + SSD_HEADS], HEAD_LANES),
             pad_lanes(wi[:, o_dt + SSD_HEADS:o_dt + 2 * SSD_HEADS], HEAD_LANES),
             wi[:, o_dt + 2 * SSD_HEADS:]], axis=1).astype(BF16)
        a_rows = pad_lanes(-jnp.exp(ssd_a_log[i]), HEAD_LANES)
        dtb_rows = pad_lanes(ssd_dt_bias[i], HEAD_LANES)
        dskip_ssd = jnp.repeat(ssd_d[i], SSD_HEAD_DIM)[None, :]
        bfull, abar, cfull = _s5_params(s5_lam_re[i], s5_lam_im[i], s5_log_step[i],
                                        s5_b_re[i], s5_b_im[i], s5_c_re[i], s5_c_im[i], nb)

        z, xbc, dt, u = _in_proj(xt, mods[i], w_pad, tm=tm, n_ctx_tiles=nb * ctx_len // tm,
                                 lat_tiles=seq // tm, ctx_row=ctx_row, s5_width=s5_width)
        ssd_args = dict(nb=nb, ctx_len=ctx_len, seq=seq, q=q)
        y_f = _ssd_pass(xbc, dt, None, None, conv_w[i], conv_b[i][None, :], a_rows[0:1],
                        dtb_rows[0:1], None, None, direction=0, **ssd_args)
        ssd_out = _ssd_pass(xbc, dt, z, y_f, conv_w[i], conv_b[i][None, :], a_rows[1:2],
                            dtb_rows[1:2], dskip_ssd, ssd_norm_g[i][None, :], direction=1,
                            **ssd_args)

        u_ctx = u[:nb * ctx_len].reshape(nb, ctx_len, s5_width).transpose(1, 0, 2)
        u_lat = u[nb * ctx_len:].reshape(nb, rows_lat, GRID_W, s5_width).transpose(2, 1, 0, 3)
        u_scan = jnp.concatenate([u_ctx.reshape(ctx_len * nb, s5_width),
                                  u_lat.reshape(seq * nb, s5_width)], axis=0)
        s5_args = dict(nb=nb, ctx_len=ctx_len, steps=steps)
        ys_f = _s5_pass(u_scan, None, bfull[0], abar[0], cfull, None, None, None,
                        direction=0, **s5_args)
        s5_scan = _s5_pass(u_scan, ys_f, bfull[1], abar[1], cfull, s5_d[i][None, :],
                           s5_glu_w[i].astype(BF16), s5_glu_b[i][None, :], direction=1,
                           **s5_args)
        s5_ctx = s5_scan[:ctx_len * nb].reshape(ctx_len, nb, s5_width).transpose(1, 0, 2)
        s5_lat = s5_scan[ctx_len * nb:].reshape(GRID_W, rows_lat, nb, s5_width)
        s5_lat = s5_lat.transpose(2, 1, 0, 3)
        s5_out = jnp.concatenate([s5_ctx.reshape(nb * ctx_len, s5_width),
                                  s5_lat.reshape(nb * seq, s5_width)], axis=0)

        wo = w_out[i].astype(BF16)
        xt = _out_proj(xt, ssd_out, s5_out, mods[i], wo[:SSD_WIDTH], wo[SSD_WIDTH:],
                       ln1_g[i][None, :], ln1_b[i][None, :], tm=tm, alpha=alpha,
                       n_ctx_tiles=nb * ctx_len // tm, lat_tiles=seq // tm, ctx_row=ctx_row)

        last = i == depth - 1
        ctx_tiles = nb * ctx_len // tp
        vt_blocks = peer_v[i].astype(BF16).reshape(nk * nk // eb, eb, d).transpose(0, 2, 1)
        xt = _peer(xt, mods[i], peer_wq[i].T.astype(BF16),
                   peer_keys[i].reshape(heads * 2 * nk, -1),
                   peer_u[i].astype(BF16), vt_blocks,
                   ln2_g[i][None, :], ln2_b[i][None, :], tm=tp, heads=heads, alpha=alpha,
                   skip_tiles=ctx_tiles if last else 0,
                   n_ctx_tiles=0 if last else ctx_tiles, lat_tiles=seq // tp, ctx_row=ctx_row)

    return xt.reshape(nb, seq, d)
```

```python
import functools
import math

import jax
import jax.numpy as jnp
from jax import lax
from jax.experimental import pallas as pl
from jax.experimental.pallas import tpu as pltpu

GRID_W = 64

SSD_HEADS = 12
SSD_HEAD_DIM = 64
SSD_GROUPS = 2
SSD_STATE = 128
SSD_WIDTH = SSD_HEADS * SSD_HEAD_DIM
XBC_WIDTH = SSD_WIDTH + 2 * SSD_GROUPS * SSD_STATE
HEAD_LANES = 128
S5_GROUP = 16
S5_STATE = 64
S5_SUB_STEPS = 64
PEER_TOPK = 16
EPS = 1e-5

F32 = jnp.float32
BF16 = jnp.bfloat16
HIGHEST = lax.Precision.HIGHEST
NEG_INF = float("-inf")

VMEM_LIMIT = 56 * 1024 * 1024


def _cparams(sem):
    return pltpu.CompilerParams(dimension_semantics=sem, vmem_limit_bytes=VMEM_LIMIT)


def _silu(x):
    return x * jax.nn.sigmoid(x)


def _softplus(x):
    return jnp.maximum(x, 0.0) + jnp.log1p(jnp.exp(-jnp.abs(x)))


def _dot_select(x, sel):
    hi = x.astype(BF16)
    r1 = x - hi.astype(F32)
    mid = r1.astype(BF16)
    lo = (r1 - mid.astype(F32)).astype(BF16)
    return (jnp.dot(hi, sel, preferred_element_type=F32)
            + jnp.dot(mid, sel, preferred_element_type=F32)
            + jnp.dot(lo, sel, preferred_element_type=F32))


def _layer_norm(y, g, b):
    mu = jnp.mean(y, axis=-1, keepdims=True)
    d = y - mu
    var = jnp.mean(d * d, axis=-1, keepdims=True)
    return d * lax.rsqrt(var + EPS) * g + b


def _mod_row(tile, n_ctx_tiles, lat_tiles_per_batch, ctx_row):
    return jnp.where(tile < n_ctx_tiles, ctx_row,
                     (tile - n_ctx_tiles) // lat_tiles_per_batch)


def _mod_kernel(c_ref, w_ref, b_ref, o_ref):
    o_ref[...] = jnp.dot(_silu(c_ref[...]), w_ref[...], precision=HIGHEST,
                         preferred_element_type=F32) + b_ref[...]


def _mod_table(cc, w_mod, b_mod):
    depth, d, d6 = w_mod.shape
    rows = cc.shape[0]
    nj = d6 // d
    return pl.pallas_call(
        _mod_kernel,
        grid=(depth, nj),
        in_specs=[pl.BlockSpec((rows, d), lambda l, j: (0, 0)),
                  pl.BlockSpec((None, d, d), lambda l, j: (l, 0, j)),
                  pl.BlockSpec((None, 1, d), lambda l, j: (l, 0, j))],
        out_specs=pl.BlockSpec((None, rows, d), lambda l, j: (l, 0, j)),
        out_shape=jax.ShapeDtypeStruct((depth, rows, d6), F32),
        compiler_params=_cparams(("arbitrary", "arbitrary")),
        name="mod_table",
    )(cc, w_mod, b_mod.reshape(depth, 1, d6))


def _in_proj_kernel(x_ref, mod_ref, w_ref, z_ref, xbc_ref, dt_ref, u_ref, *,
                    d, n_ctx_tiles, lat_tiles, ctx_row):
    row = _mod_row(pl.program_id(0), n_ctx_tiles, lat_tiles, ctx_row)
    m0 = mod_ref[pl.ds(row, 1), 0:d]
    m1 = mod_ref[pl.ds(row, 1), d:2 * d]
    xm = (x_ref[...] * (1.0 + m1) + m0).astype(BF16)
    p = jnp.dot(xm, w_ref[...], preferred_element_type=F32)
    o = 0
    for ref in (z_ref, xbc_ref, dt_ref, u_ref):
        w = ref.shape[-1]
        ref[...] = p[:, o:o + w]
        o += w


def _in_proj(x, mod, w, *, tm, n_ctx_tiles, lat_tiles, ctx_row, s5_width):
    t, d = x.shape
    widths = (SSD_WIDTH, XBC_WIDTH, 2 * HEAD_LANES, s5_width)
    kern = functools.partial(_in_proj_kernel, d=d, n_ctx_tiles=n_ctx_tiles,
                             lat_tiles=lat_tiles, ctx_row=ctx_row)
    return pl.pallas_call(
        kern,
        grid=(t // tm,),
        in_specs=[pl.BlockSpec((tm, d), lambda i: (i, 0)),
                  pl.BlockSpec(mod.shape, lambda i: (0, 0)),
                  pl.BlockSpec(w.shape, lambda i: (0, 0))],
        out_specs=[pl.BlockSpec((tm, wd), lambda i: (i, 0)) for wd in widths],
        out_shape=[jax.ShapeDtypeStruct((t, wd), F32) for wd in widths],
        compiler_params=_cparams(("arbitrary",)),
        name="in_proj",
    )(x, mod, w)


def _ssd_chunk_index(s, *, nctx, nlat, reverse):
    if not reverse:
        return s
    return jnp.where(s < nctx, nctx - 1 - s, nctx + nlat - 1 - (s - nctx))


def _ssd_row_block(b, c, *, nb, nctx, nlat):
    return jnp.where(c < nctx, b * nctx + c, nb * nctx + b * nlat + (c - nctx))


def _ssd_kernel(*refs, q, nctx, nlat, reverse, final):
    if final:
        (xbc_ref, prev_ref, next_ref, dt_ref, z_ref, yprev_ref, convw_ref, convb_ref,
         a_ref, dtb_ref, dskip_ref, ng_ref, out_ref, h_ref) = refs
    else:
        (xbc_ref, prev_ref, next_ref, dt_ref, convw_ref, convb_ref,
         a_ref, dtb_ref, out_ref, h_ref) = refs
    s = pl.program_id(1)
    c = _ssd_chunk_index(s, nctx=nctx, nlat=nlat, reverse=reverse)

    @pl.when(s == 0)
    def _():
        h_ref[...] = jnp.zeros_like(h_ref)

    x = xbc_ref[...]
    rows = lax.broadcasted_iota(jnp.int32, x.shape, 0)
    has_prev = jnp.logical_and(c != 0, c != nctx).astype(F32)
    has_next = jnp.logical_and(c != nctx - 1, c != nctx + nlat - 1).astype(F32)
    xm1 = jnp.where(rows == 0, prev_ref[7:8, :] * has_prev, pltpu.roll(x, 1, 0))
    xp1 = jnp.where(rows == q - 1, next_ref[0:1, :] * has_next, pltpu.roll(x, q - 1, 0))
    conv = (convw_ref[0:1, :] * xm1 + convw_ref[1:2, :] * x + convw_ref[2:3, :] * xp1
            + convb_ref[...])
    act = _silu(conv)
    xs = act[:, :SSD_WIDTH]
    bm = act[:, SSD_WIDTH:SSD_WIDTH + SSD_GROUPS * SSD_STATE]
    cm = act[:, SSD_WIDTH + SSD_GROUPS * SSD_STATE:]

    dt = _softplus(dt_ref[...] + dtb_ref[...])
    a = dt * a_ref[...]
    li = lax.broadcasted_iota(jnp.int32, (q, q), 0)
    si = lax.broadcasted_iota(jnp.int32, (q, q), 1)
    mask = (si >= li) if reverse else (si <= li)
    mask_t = (li >= si) if reverse else (li <= si)
    cum_t = _dot_select(a.T, mask_t.astype(BF16))
    cum = cum_t.T
    edge = 0 if reverse else q - 1

    hh = lax.broadcasted_iota(jnp.int32, (HEAD_LANES, SSD_WIDTH), 0)
    ll = lax.broadcasted_iota(jnp.int32, (HEAD_LANES, SSD_WIDTH), 1)
    expand = (lax.shift_right_logical(ll, int(math.log2(SSD_HEAD_DIM))) == hh).astype(BF16)
    dt_e = _dot_select(dt, expand)
    cum_e = _dot_select(cum, expand)
    tot_e = cum_e[edge:edge + 1, :]
    xdt = xs * dt_e
    xdec = (xdt * jnp.exp(tot_e - cum_e)).astype(BF16)
    off_scale = jnp.exp(cum_e)

    gw = SSD_WIDTH // SSD_GROUPS
    heads_per_group = SSD_HEADS // SSD_GROUPS
    lane = lax.broadcasted_iota(jnp.int32, (q, 2 * SSD_HEAD_DIM), 1)
    h_old = h_ref[...]
    y_groups = []
    st_groups = []
    for g in range(SSD_GROUPS):
        bg = bm[:, g * SSD_STATE:(g + 1) * SSD_STATE]
        cg = cm[:, g * SSD_STATE:(g + 1) * SSD_STATE].astype(BF16)
        cb = lax.dot_general(cg, bg.astype(BF16), (((1,), (1,)), ((), ())),
                             preferred_element_type=F32)
        pairs = []
        for pr in range(heads_per_group // 2):
            lo = g * gw + pr * 2 * SSD_HEAD_DIM
            xp = xdt[:, lo:lo + 2 * SSD_HEAD_DIM]
            x_bd = jnp.concatenate([jnp.where(lane < SSD_HEAD_DIM, xp, 0.0),
                                    jnp.where(lane < SSD_HEAD_DIM, 0.0, xp)], axis=0)
            ms = []
            for k in range(2):
                h = g * heads_per_group + pr * 2 + k
                seg = cum[:, h:h + 1] - cum_t[h:h + 1, :]
                lm = jnp.exp(jnp.where(mask, seg, NEG_INF))
                ms.append((cb * lm).astype(BF16))
            pairs.append(jnp.dot(jnp.concatenate(ms, axis=1), x_bd.astype(BF16),
                                 preferred_element_type=F32))
        y_diag = jnp.concatenate(pairs, axis=1)
        hg = h_old[:, g * gw:(g + 1) * gw]
        y_off = jnp.dot(cg, hg.astype(BF16), preferred_element_type=F32)
        y_groups.append(y_diag + y_off * off_scale[:, g * gw:(g + 1) * gw])
        st_groups.append(jnp.dot(bg.T.astype(BF16), xdec[:, g * gw:(g + 1) * gw],
                                 preferred_element_type=F32))
    y = jnp.concatenate(y_groups, axis=1)
    h_ref[...] = h_old * jnp.exp(tot_e) + jnp.concatenate(st_groups, axis=1)

    if final:
        y = y + yprev_ref[...] + xs * dskip_ref[...]
        gated = y * _silu(z_ref[...])
        ms = jnp.mean(gated * gated, axis=-1, keepdims=True)
        out_ref[...] = gated * lax.rsqrt(ms + EPS) * ng_ref[...]
    else:
        out_ref[...] = y


def _ssd_pass(xbc, dt, z, yprev, convw, convb, a_row, dtb_row, dskip, ng, *,
              nb, ctx_len, seq, q, direction):
    t = xbc.shape[0]
    nctx, nlat = ctx_len // q, seq // q
    reverse = direction == 1
    final = direction == 1
    q8 = q // 8
    last8 = t // 8 - 1

    def blk(b, s):
        c = _ssd_chunk_index(s, nctx=nctx, nlat=nlat, reverse=reverse)
        return _ssd_row_block(b, c, nb=nb, nctx=nctx, nlat=nlat)

    row_map = lambda b, s: (blk(b, s), 0)
    prev_map = lambda b, s: (jnp.maximum(blk(b, s) * q8 - 1, 0), 0)
    next_map = lambda b, s: (jnp.minimum((blk(b, s) + 1) * q8, last8), 0)
    dt_map = lambda b, s: (blk(b, s), direction)
    const = lambda b, s: (0, 0)

    in_specs = [pl.BlockSpec((q, XBC_WIDTH), row_map),
                pl.BlockSpec((8, XBC_WIDTH), prev_map),
                pl.BlockSpec((8, XBC_WIDTH), next_map),
                pl.BlockSpec((q, HEAD_LANES), dt_map)]
    args = [xbc, xbc, xbc, dt]
    if final:
        in_specs += [pl.BlockSpec((q, SSD_WIDTH), row_map)] * 2
        args += [z, yprev]
    in_specs += [pl.BlockSpec(convw.shape, const), pl.BlockSpec(convb.shape, const),
                 pl.BlockSpec(a_row.shape, const), pl.BlockSpec(dtb_row.shape, const)]
    args += [convw, convb, a_row, dtb_row]
    if final:
        in_specs += [pl.BlockSpec(dskip.shape, const), pl.BlockSpec(ng.shape, const)]
        args += [dskip, ng]
    kern = functools.partial(_ssd_kernel, q=q, nctx=nctx, nlat=nlat,
                             reverse=reverse, final=final)
    return pl.pallas_call(
        kern,
        grid=(nb, nctx + nlat),
        in_specs=in_specs,
        out_specs=pl.BlockSpec((q, SSD_WIDTH), row_map),
        out_shape=jax.ShapeDtypeStruct((t, SSD_WIDTH), F32),
        scratch_shapes=[pltpu.VMEM((SSD_STATE, SSD_WIDTH), F32)],
        compiler_params=_cparams(("arbitrary", "arbitrary")),
        name="ssd_bwd" if final else "ssd_fwd",
    )(*args)


def _s5_block_index(i, *, nctx, nblk, reverse):
    if not reverse:
        return i
    return jnp.where(i < nctx, nctx - 1 - i, nblk - 1 - (i - nctx))


def _s5_kernel(*refs, steps, nb, reverse, final):
    if final:
        (u_ref, yprev_ref, bfull_ref, abar_ref, cfull_ref, dskip_ref, gluw_ref, glub_ref,
         out_ref, state_ref, x_ref) = refs
    else:
        u_ref, bfull_ref, abar_ref, cfull_ref, out_ref, state_ref, x_ref = refs

    @pl.when(pl.program_id(0) == 0)
    def _():
        state_ref[...] = jnp.zeros_like(state_ref)

    n = abar_ref.shape[1] // 2
    half = n // 2
    sub = S5_SUB_STEPS
    order = list(range(steps // sub))
    if reverse:
        order.reverse()

    def rows(s):
        return slice(s * sub * nb, (s + 1) * sub * nb)

    def project_in(s):
        x_ref[rows(s), :] = jnp.dot(u_ref[rows(s), :].astype(BF16), bfull_ref[...],
                                    preferred_element_type=F32)

    def scan(s):
        for part in range(2):
            re = slice(part * half, (part + 1) * half)
            im = slice(n + part * half, n + (part + 1) * half)
            ar = abar_ref[:, re]
            ai = abar_ref[:, im]
            xr = state_ref[:, re]
            xi = state_ref[:, im]
            for k in range(sub):
                kk = sub - 1 - k if reverse else k
                r = slice((s * sub + kk) * nb, (s * sub + kk + 1) * nb)
                xr, xi = (ar * xr - ai * xi + x_ref[r, re], ar * xi + ai * xr + x_ref[r, im])
                x_ref[r, re] = xr
                x_ref[r, im] = xi
            state_ref[:, re] = xr
            state_ref[:, im] = xi

    def read_out(s):
        y = jnp.dot(x_ref[rows(s), :].astype(BF16), cfull_ref[...],
                    preferred_element_type=F32)
        if final:
            y = jax.nn.gelu(y + yprev_ref[rows(s), :] + u_ref[rows(s), :] * dskip_ref[...])
            gate = jnp.dot(y.astype(BF16), gluw_ref[...], preferred_element_type=F32)
            y = y * jax.nn.sigmoid(gate + glub_ref[...])
        out_ref[rows(s), :] = y

    project_in(order[0])
    for idx, s in enumerate(order):
        if idx + 1 < len(order):
            project_in(order[idx + 1])
        if idx > 0:
            read_out(order[idx - 1])
        scan(s)
    read_out(order[-1])


def _s5_pass(u_scan, yprev, bfull, abar, cfull, dskip, gluw, glub, *,
             nb, ctx_len, steps, direction):
    rows, width = u_scan.shape
    nblk = rows // (steps * nb)
    nctx = ctx_len // steps
    reverse = direction == 1
    final = direction == 1
    row_map = lambda i: (_s5_block_index(i, nctx=nctx, nblk=nblk, reverse=reverse), 0)
    const = lambda i: (0, 0)
    blk = pl.BlockSpec((steps * nb, width), row_map)
    in_specs = [blk]
    args = [u_scan]
    if final:
        in_specs.append(blk)
        args.append(yprev)
    in_specs += [pl.BlockSpec(bfull.shape, const), pl.BlockSpec(abar.shape, const),
                 pl.BlockSpec(cfull.shape, const)]
    args += [bfull, abar, cfull]
    if final:
        in_specs += [pl.BlockSpec(dskip.shape, const), pl.BlockSpec(gluw.shape, const),
                     pl.BlockSpec(glub.shape, const)]
        args += [dskip, gluw, glub]
    kern = functools.partial(_s5_kernel, steps=steps, nb=nb, reverse=reverse, final=final)
    return pl.pallas_call(
        kern,
        grid=(nblk,),
        in_specs=in_specs,
        out_specs=blk,
        out_shape=jax.ShapeDtypeStruct((rows, width), F32),
        scratch_shapes=[pltpu.VMEM(abar.shape, F32),
                        pltpu.VMEM((steps * nb, abar.shape[1]), F32)],
        compiler_params=_cparams(("arbitrary",)),
        name="s5_bwd" if final else "s5_fwd",
    )(*args)


def _s5_params(lam_re, lam_im, log_step, b_re, b_im, c_re, c_im, nb):
    g, p = lam_re.shape[1], lam_re.shape[2]
    step = jnp.exp(log_step)[..., None]
    mag = jnp.exp(lam_re * step)
    ar = mag * jnp.cos(lam_im * step)
    ai = mag * jnp.sin(lam_im * step)
    den = lam_re * lam_re + lam_im * lam_im
    kr = ((ar - 1.0) * lam_re + ai * lam_im) / den
    ki = (ai * lam_re - (ar - 1.0) * lam_im) / den
    bbr = kr[..., None] * b_re - ki[..., None] * b_im
    bbi = kr[..., None] * b_im + ki[..., None] * b_re
    eye = jnp.eye(g, dtype=F32)
    cdim = b_re.shape[-1]

    def bmat(bb):
        return jnp.einsum('gpc,gh->gchp', bb, eye).reshape(g * cdim, g * p)

    bfull = jnp.stack([jnp.concatenate([bmat(bbr[d]), bmat(bbi[d])], axis=1)
                       for d in range(2)]).astype(BF16)
    abar = jnp.stack([jnp.broadcast_to(
        jnp.concatenate([ar[d].reshape(-1), ai[d].reshape(-1)])[None, :], (nb, 2 * g * p))
        for d in range(2)])

    def cmat(cc):
        return jnp.einsum('gcp,gh->gphc', cc, eye).reshape(g * p, g * cdim)

    cfull = jnp.concatenate([cmat(c_re), -cmat(c_im)], axis=0).astype(BF16)
    return bfull, abar, cfull


def _out_proj_kernel(x_ref, ssd_ref, s5_ref, mod_ref, w1_ref, w2_ref, g_ref, b_ref, o_ref, *,
                     d, alpha, n_ctx_tiles, lat_tiles, ctx_row):
    row = _mod_row(pl.program_id(0), n_ctx_tiles, lat_tiles, ctx_row)
    m2 = mod_ref[pl.ds(row, 1), 2 * d:3 * d]
    mix = (jnp.dot(ssd_ref[...].astype(BF16), w1_ref[...], preferred_element_type=F32)
           + jnp.dot(s5_ref[...].astype(BF16), w2_ref[...], preferred_element_type=F32))
    o_ref[...] = _layer_norm(alpha * x_ref[...] + m2 * mix, g_ref[...], b_ref[...])


def _out_proj(x, ssd, s5, mod, w1, w2, g, b, *, tm, alpha, n_ctx_tiles, lat_tiles, ctx_row):
    t, d = x.shape
    kern = functools.partial(_out_proj_kernel, d=d, alpha=alpha, n_ctx_tiles=n_ctx_tiles,
                             lat_tiles=lat_tiles, ctx_row=ctx_row)
    const = lambda i: (0, 0)
    return pl.pallas_call(
        kern,
        grid=(t // tm,),
        in_specs=[pl.BlockSpec((tm, d), lambda i: (i, 0)),
                  pl.BlockSpec((tm, ssd.shape[1]), lambda i: (i, 0)),
                  pl.BlockSpec((tm, s5.shape[1]), lambda i: (i, 0)),
                  pl.BlockSpec(mod.shape, const), pl.BlockSpec(w1.shape, const),
                  pl.BlockSpec(w2.shape, const), pl.BlockSpec(g.shape, const),
                  pl.BlockSpec(b.shape, const)],
        out_specs=pl.BlockSpec((tm, d), lambda i: (i, 0)),
        out_shape=jax.ShapeDtypeStruct((t, d), F32),
        compiler_params=_cparams(("arbitrary",)),
        name="out_proj_ln",
    )(x, ssd, s5, mod, w1, w2, g, b)


def _peer_kernel(x_ref, mod_ref, wqt_ref, keys_ref, u_ref, vt_ref, g_ref, b_ref, o_ref,
                 hbt_ref, acc_ref, q_ref, top_ref, cand_ref, ord_ref, work_ref, s0_ref, s1_ref,
                 rnk_ref, rank_ref, e1_ref, n_ref, r_ref, *,
                 d, nk, heads, half, alpha, n_ctx_tiles, lat_tiles, ctx_row):
    i = pl.program_id(0)
    j = pl.program_id(1)
    tm = x_ref.shape[0]
    eb = u_ref.shape[0]
    nsel = PEER_TOPK + 1
    pad_rows = top_ref.shape[1]
    pk = 16

    @pl.when(j == 0)
    def _prepare():
        row = _mod_row(i, n_ctx_tiles, lat_tiles, ctx_row)
        m3 = mod_ref[pl.ds(row, 1), 3 * d:4 * d]
        m4 = mod_ref[pl.ds(row, 1), 4 * d:5 * d]
        h = x_ref[...] * (1.0 + m4) + m3
        hbt = h.T.astype(BF16)
        hbt_ref[...] = hbt
        q_ref[...] = jnp.dot(wqt_ref[...], hbt, preferred_element_type=F32)
        acc_ref[...] = jnp.zeros_like(acc_ref)

        def per_head(hd, carry):
            top_ref[...] = jnp.full(top_ref.shape, NEG_INF, F32)
            rnk_ref[...] = jnp.full(rnk_ref.shape, 2.0 * nsel, F32)
            for c, dst in ((0, s0_ref), (1, s1_ref)):
                off = pl.multiple_of((hd * 2 + c) * nk, nk)
                qoff = pl.multiple_of((hd * 2 + c) * half, half)
                sc = jnp.dot(keys_ref[pl.ds(off, nk), :], q_ref[pl.ds(qoff, half), :],
                             precision=HIGHEST, preferred_element_type=F32)
                dst[...] = sc
                work_ref[c] = sc

            def take(r, carry):
                for c in range(2):
                    cur = work_ref[c]
                    m = jnp.max(cur, axis=0, keepdims=True)
                    top_ref[c, pl.ds(r, 1), :] = m
                    hit = cur >= m
                    work_ref[c] = jnp.where(hit, NEG_INF, cur)
                    if c == 1:
                        rnk_ref[...] = jnp.where(hit, lax.convert_element_type(r + 1, F32),
                                                 rnk_ref[...])
                return carry

            lax.fori_loop(0, nsel, take, 0)
            v0 = top_ref[0]
            v1 = top_ref[1]
            cands = [v0[0:1, :] + v1]
            cands += [v0[a:a + 1, :] + v1[0:8, :] for a in range(1, 8)]
            cands += [v0[8:pad_rows, :] + v1[0:1, :]]
            cand = jnp.concatenate(cands, axis=0)
            cand_ref[...] = cand

            def take2(r, carry):
                cur = cand_ref[...]
                m = jnp.max(cur, axis=0, keepdims=True)
                ord_ref[pl.ds(r, 1), :] = m
                cand_ref[...] = jnp.where(cur >= m, NEG_INF, cur)
                return carry

            lax.fori_loop(0, nsel, take2, 0)
            tau = 0.5 * (ord_ref[nsel - 2:nsel - 1, :] + ord_ref[nsel - 1:nsel, :])
            top = v0[0:1, :] + v1[0:1, :]
            z = jnp.sum(jnp.where(cand > tau, jnp.exp(cand - top), 0.0), axis=0, keepdims=True)
            s0 = s0_ref[...]
            theta = tau - s0
            count = jnp.zeros_like(theta)
            for bb in range(PEER_TOPK):
                count = count + (v1[bb:bb + 1, :] >= theta).astype(F32)
            n_ref[hd] = count
            r_ref[hd] = jnp.exp(s0 - v0[0:1, :]) / z
            e1 = jnp.exp(s1_ref[...] - v1[0:1, :])
            e1_ref[hd] = e1.astype(BF16).reshape(nk // pk, pk, tm)
            rank_ref[hd] = rnk_ref[...].astype(BF16).reshape(nk // pk, pk, tm)
            return carry

        lax.fori_loop(0, heads, per_head, 0)

    lanes = 128
    keys_per_step = eb // nk
    sub_keys = 4
    base = pl.multiple_of(j * keys_per_step, keys_per_step)
    hbt = hbt_ref[...]
    sub_rows = sub_keys * nk
    n_sub = keys_per_step // sub_keys

    def expert_acts(sub):
        return jnp.dot(u_ref[sub * sub_rows:(sub + 1) * sub_rows, :], hbt,
                       preferred_element_type=F32)

    def project(sub, gate):
        acc_ref[...] += jnp.dot(vt_ref[:, sub * sub_rows:(sub + 1) * sub_rows], gate,
                                preferred_element_type=F32)

    a_next = expert_acts(0)
    gate_prev = None
    for sub in range(n_sub):
        a_sub = a_next
        if sub + 1 < n_sub:
            a_next = expert_acts(sub + 1)
        if gate_prev is not None:
            project(sub - 1, gate_prev)
        rows = []
        for k2 in range(sub_keys):
            ii = sub * sub_keys + k2
            cols = []
            for lt in range(tm // lanes):
                ls = slice(lt * lanes, (lt + 1) * lanes)
                w = jnp.zeros((nk // pk, pk, lanes), BF16)
                for hd in range(heads):
                    cnt = n_ref[hd, pl.ds(base, keys_per_step), ls][ii:ii + 1, :]
                    rr = r_ref[hd, pl.ds(base, keys_per_step), ls][ii:ii + 1, :]
                    cnt = jnp.broadcast_to(cnt, (pk, lanes)).astype(BF16)[None]
                    rr = jnp.broadcast_to(rr, (pk, lanes)).astype(BF16)[None]
                    w = w + jnp.where(rank_ref[hd, :, :, ls] <= cnt, e1_ref[hd, :, :, ls],
                                      jnp.zeros((), BF16)) * rr
                a = a_sub[k2 * nk:(k2 + 1) * nk, ls].astype(BF16).reshape(nk // pk, pk, lanes)
                cols.append((w * jax.nn.gelu(a)).reshape(nk, lanes))
            rows.append(jnp.concatenate(cols, axis=1))
        gate_prev = jnp.concatenate(rows, axis=0)
    project(n_sub - 1, gate_prev)

    @pl.when(j == pl.num_programs(1) - 1)
    def _finish():
        row = _mod_row(i, n_ctx_tiles, lat_tiles, ctx_row)
        m5 = mod_ref[pl.ds(row, 1), 5 * d:6 * d]
        f = acc_ref[...].T
        o_ref[...] = _layer_norm(alpha * x_ref[...] + m5 * f, g_ref[...], b_ref[...])


def _peer(x, mod, wqt, keys2d, u_b, vt_b, g, b, *, tm, heads, alpha,
          skip_tiles, n_ctx_tiles, lat_tiles, ctx_row):
    t, d = x.shape
    nblk, _, eb = vt_b.shape
    nk = keys2d.shape[0] // (2 * heads)
    half = keys2d.shape[1]
    pad_rows = 24
    ncand = pad_rows + 7 * 8 + (pad_rows - 8)
    kern = functools.partial(_peer_kernel, d=d, nk=nk, heads=heads, half=half, alpha=alpha,
                             n_ctx_tiles=n_ctx_tiles, lat_tiles=lat_tiles, ctx_row=ctx_row)
    const = lambda i, j: (0, 0)
    return pl.pallas_call(
        kern,
        grid=(t // tm - skip_tiles, nblk),
        in_specs=[pl.BlockSpec((tm, d), lambda i, j: (i + skip_tiles, 0)),
                  pl.BlockSpec(mod.shape, const), pl.BlockSpec(wqt.shape, const),
                  pl.BlockSpec(keys2d.shape, const),
                  pl.BlockSpec((eb, d), lambda i, j: (j, 0)),
                  pl.BlockSpec((None, d, eb), lambda i, j: (j, 0, 0)),
                  pl.BlockSpec(g.shape, const), pl.BlockSpec(b.shape, const)],
        out_specs=pl.BlockSpec((tm, d), lambda i, j: (i, 0)),
        out_shape=jax.ShapeDtypeStruct((t - skip_tiles * tm, d), F32),
        scratch_shapes=[pltpu.VMEM((d, tm), BF16),
                        pltpu.VMEM((d, tm), F32),
                        pltpu.VMEM((wqt.shape[0], tm), F32),
                        pltpu.VMEM((2, pad_rows, tm), F32),
                        pltpu.VMEM((ncand, tm), F32),
                        pltpu.VMEM((pad_rows, tm), F32),
                        pltpu.VMEM((2, nk, tm), F32),
                        pltpu.VMEM((nk, tm), F32),
                        pltpu.VMEM((nk, tm), F32),
                        pltpu.VMEM((nk, tm), F32),
                        pltpu.VMEM((heads, nk // 16, 16, tm), BF16),
                        pltpu.VMEM((heads, nk // 16, 16, tm), BF16),
                        pltpu.VMEM((heads, nk, tm), F32),
                        pltpu.VMEM((heads, nk, tm), F32)],
        compiler_params=_cparams(("arbitrary", "arbitrary")),
        name="peer_ln",
    )(x, mod, wqt, keys2d, u_b, vt_b, g, b)


def _pick_tile(limit, *sizes):
    tm = limit
    while any(s % tm for s in sizes):
        tm //= 2
    return tm


def kernel(x, c, ctx, c_ctx, w_mod, b_mod, w_in, conv_w, conv_b, ssd_a_log, ssd_dt_bias, ssd_d, ssd_norm_g, s5_lam_re, s5_lam_im, s5_log_step, s5_b_re, s5_b_im, s5_c_re, s5_c_im, s5_d, s5_glu_w, s5_glu_b, w_out, ln1_g, ln1_b, peer_wq, peer_keys, peer_u, peer_v, ln2_g, ln2_b):
    nb, seq, d = x.shape
    ctx_len = ctx.shape[1]
    depth = w_mod.shape[0]
    s5_width = s5_d.shape[1]
    heads = peer_keys.shape[1]
    nk = peer_keys.shape[3]
    alpha = (2.0 * depth) ** 0.25
    rows_lat = seq // GRID_W

    tm = _pick_tile(512, nb * ctx_len, seq)
    tp = _pick_tile(512, nb * ctx_len, seq)
    q = _pick_tile(128, ctx_len, seq)
    steps = _pick_tile(256, ctx_len, seq)
    eb = 16 * nk
    ctx_row = nb

    mod_rows = -(-(nb + 1) // 8) * 8
    cc = jnp.concatenate([c, c_ctx[None, :], jnp.zeros((mod_rows - nb - 1, d), F32)], axis=0)
    mods = _mod_table(cc, w_mod, b_mod)

    xt = jnp.concatenate([ctx.reshape(nb * ctx_len, d), x.reshape(nb * seq, d)], axis=0)

    def pad_lanes(w, n):
        return jnp.pad(w, ((0, 0), (0, n - w.shape[1])))

    for i in range(depth):
        wi = w_in[i]
        o_dt = SSD_WIDTH + XBC_WIDTH
        w_pad = jnp.concatenate(
            [wi[:, :o_dt],
             pad_lanes(wi[:, o_dt:o_dt + SSD_HEADS], HEAD_LANES),
             pad_lanes(wi[:, o_dt + SSD_HEADS:o_dt + 2 * SSD_HEADS], HEAD_LANES),
             wi[:, o_dt + 2 * SSD_HEADS:]], axis=1).astype(BF16)
        a_rows = pad_lanes(-jnp.exp(ssd_a_log[i]), HEAD_LANES)
        dtb_rows = pad_lanes(ssd_dt_bias[i], HEAD_LANES)
        dskip_ssd = jnp.repeat(ssd_d[i], SSD_HEAD_DIM)[None, :]
        bfull, abar, cfull = _s5_params(s5_lam_re[i], s5_lam_im[i], s5_log_step[i],
                                        s5_b_re[i], s5_b_im[i], s5_c_re[i], s5_c_im[i], nb)

        z, xbc, dt, u = _in_proj(xt, mods[i], w_pad, tm=tm, n_ctx_tiles=nb * ctx_len // tm,
                                 lat_tiles=seq // tm, ctx_row=ctx_row, s5_width=s5_width)
        ssd_args = dict(nb=nb, ctx_len=ctx_len, seq=seq, q=q)
        y_f = _ssd_pass(xbc, dt, None, None, conv_w[i], conv_b[i][None, :], a_rows[0:1],
                        dtb_rows[0:1], None, None, direction=0, **ssd_args)
        ssd_out = _ssd_pass(xbc, dt, z, y_f, conv_w[i], conv_b[i][None, :], a_rows[1:2],
                            dtb_rows[1:2], dskip_ssd, ssd_norm_g[i][None, :], direction=1,
                            **ssd_args)

        u_ctx = u[:nb * ctx_len].reshape(nb, ctx_len, s5_width).transpose(1, 0, 2)
        u_lat = u[nb * ctx_len:].reshape(nb, rows_lat, GRID_W, s5_width).transpose(2, 1, 0, 3)
        u_scan = jnp.concatenate([u_ctx.reshape(ctx_len * nb, s5_width),
                                  u_lat.reshape(seq * nb, s5_width)], axis=0)
        s5_args = dict(nb=nb, ctx_len=ctx_len, steps=steps)
        ys_f = _s5_pass(u_scan, None, bfull[0], abar[0], cfull, None, None, None,
                        direction=0, **s5_args)
        s5_scan = _s5_pass(u_scan, ys_f, bfull[1], abar[1], cfull, s5_d[i][None, :],
                           s5_glu_w[i].astype(BF16), s5_glu_b[i][None, :], direction=1,
                           **s5_args)
        s5_ctx = s5_scan[:ctx_len * nb].reshape(ctx_len, nb, s5_width).transpose(1, 0, 2)
        s5_lat = s5_scan[ctx_len * nb:].reshape(GRID_W, rows_lat, nb, s5_width)
        s5_lat = s5_lat.transpose(2, 1, 0, 3)
        s5_out = jnp.concatenate([s5_ctx.reshape(nb * ctx_len, s5_width),
                                  s5_lat.reshape(nb * seq, s5_width)], axis=0)

        wo = w_out[i].astype(BF16)
        xt = _out_proj(xt, ssd_out, s5_out, mods[i], wo[:SSD_WIDTH], wo[SSD_WIDTH:],
                       ln1_g[i][None, :], ln1_b[i][None, :], tm=tm, alpha=alpha,
                       n_ctx_tiles=nb * ctx_len // tm, lat_tiles=seq // tm, ctx_row=ctx_row)

        last = i == depth - 1
        ctx_tiles = nb * ctx_len // tp
        vt_blocks = peer_v[i].astype(BF16).reshape(nk * nk // eb, eb, d).transpose(0, 2, 1)
        xt = _peer(xt, mods[i], peer_wq[i].T.astype(BF16),
                   peer_keys[i].reshape(heads * 2 * nk, -1),
                   peer_u[i].astype(BF16), vt_blocks,
                   ln2_g[i][None, :], ln2_b[i][None, :], tm=tp, heads=heads, alpha=alpha,
                   skip_tiles=ctx_tiles if last else 0,
                   n_ctx_tiles=0 if last else ctx_tiles, lat_tiles=seq // tp, ctx_row=ctx_row)

    return xt.reshape(nb, seq, d)
```

```python
import functools
import math

import jax
import jax.numpy as jnp
from jax import lax
from jax.experimental import pallas as pl
from jax.experimental.pallas import tpu as pltpu

GRID_W = 64

SSD_HEADS = 12
SSD_HEAD_DIM = 64
SSD_GROUPS = 2
SSD_STATE = 128
SSD_WIDTH = SSD_HEADS * SSD_HEAD_DIM
XBC_WIDTH = SSD_WIDTH + 2 * SSD_GROUPS * SSD_STATE
HEAD_LANES = 128
S5_GROUP = 16
S5_STATE = 64
S5_SUB_STEPS = 64
PEER_TOPK = 16
EPS = 1e-5

F32 = jnp.float32
BF16 = jnp.bfloat16
HIGHEST = lax.Precision.HIGHEST
NEG_INF = float("-inf")

VMEM_LIMIT = 56 * 1024 * 1024


def _cparams(sem):
    return pltpu.CompilerParams(dimension_semantics=sem, vmem_limit_bytes=VMEM_LIMIT)


def _silu(x):
    return x * jax.nn.sigmoid(x)


def _softplus(x):
    return jnp.maximum(x, 0.0) + jnp.log1p(jnp.exp(-jnp.abs(x)))


def _dot_select(x, sel):
    hi = x.astype(BF16)
    r1 = x - hi.astype(F32)
    mid = r1.astype(BF16)
    lo = (r1 - mid.astype(F32)).astype(BF16)
    return (jnp.dot(hi, sel, preferred_element_type=F32)
            + jnp.dot(mid, sel, preferred_element_type=F32)
            + jnp.dot(lo, sel, preferred_element_type=F32))


def _layer_norm(y, g, b):
    mu = jnp.mean(y, axis=-1, keepdims=True)
    d = y - mu
    var = jnp.mean(d * d, axis=-1, keepdims=True)
    return d * lax.rsqrt(var + EPS) * g + b


def _mod_row(tile, n_ctx_tiles, lat_tiles_per_batch, ctx_row):
    return jnp.where(tile < n_ctx_tiles, ctx_row,
                     (tile - n_ctx_tiles) // lat_tiles_per_batch)


def _mod_kernel(c_ref, w_ref, b_ref, o_ref):
    o_ref[...] = jnp.dot(_silu(c_ref[...]), w_ref[...], precision=HIGHEST,
                         preferred_element_type=F32) + b_ref[...]


def _mod_table(cc, w_mod, b_mod):
    depth, d, d6 = w_mod.shape
    rows = cc.shape[0]
    nj = d6 // d
    return pl.pallas_call(
        _mod_kernel,
        grid=(depth, nj),
        in_specs=[pl.BlockSpec((rows, d), lambda l, j: (0, 0)),
                  pl.BlockSpec((None, d, d), lambda l, j: (l, 0, j)),
                  pl.BlockSpec((None, 1, d), lambda l, j: (l, 0, j))],
        out_specs=pl.BlockSpec((None, rows, d), lambda l, j: (l, 0, j)),
        out_shape=jax.ShapeDtypeStruct((depth, rows, d6), F32),
        compiler_params=_cparams(("arbitrary", "arbitrary")),
        name="mod_table",
    )(cc, w_mod, b_mod.reshape(depth, 1, d6))


def _in_proj_kernel(x_ref, mod_ref, w_ref, z_ref, xbc_ref, dt_ref, u_ref, *,
                    d, n_ctx_tiles, lat_tiles, ctx_row):
    row = _mod_row(pl.program_id(0), n_ctx_tiles, lat_tiles, ctx_row)
    m0 = mod_ref[pl.ds(row, 1), 0:d]
    m1 = mod_ref[pl.ds(row, 1), d:2 * d]
    xm = (x_ref[...] * (1.0 + m1) + m0).astype(BF16)
    p = jnp.dot(xm, w_ref[...], preferred_element_type=F32)
    o = 0
    for ref in (z_ref, xbc_ref, dt_ref, u_ref):
        w = ref.shape[-1]
        ref[...] = p[:, o:o + w]
        o += w


def _in_proj(x, mod, w, *, tm, n_ctx_tiles, lat_tiles, ctx_row, s5_width):
    t, d = x.shape
    widths = (SSD_WIDTH, XBC_WIDTH, 2 * HEAD_LANES, s5_width)
    kern = functools.partial(_in_proj_kernel, d=d, n_ctx_tiles=n_ctx_tiles,
                             lat_tiles=lat_tiles, ctx_row=ctx_row)
    return pl.pallas_call(
        kern,
        grid=(t // tm,),
        in_specs=[pl.BlockSpec((tm, d), lambda i: (i, 0)),
                  pl.BlockSpec(mod.shape, lambda i: (0, 0)),
                  pl.BlockSpec(w.shape, lambda i: (0, 0))],
        out_specs=[pl.BlockSpec((tm, wd), lambda i: (i, 0)) for wd in widths],
        out_shape=[jax.ShapeDtypeStruct((t, wd), F32) for wd in widths],
        compiler_params=_cparams(("arbitrary",)),
        name="in_proj",
    )(x, mod, w)


def _ssd_chunk_index(s, *, nctx, nlat, reverse):
    if not reverse:
        return s
    return jnp.where(s < nctx, nctx - 1 - s, nctx + nlat - 1 - (s - nctx))


def _ssd_row_block(b, c, *, nb, nctx, nlat):
    return jnp.where(c < nctx, b * nctx + c, nb * nctx + b * nlat + (c - nctx))


def _ssd_kernel(*refs, q, n_sub, nctx, nlat, reverse, final):
    if final:
        (xbc_ref, prev_ref, next_ref, dt_ref, z_ref, yprev_ref, convw_ref, convb_ref,
         a_ref, dtb_ref, dskip_ref, ng_ref, out_ref, h_ref) = refs
    else:
        (xbc_ref, prev_ref, next_ref, dt_ref, convw_ref, convb_ref,
         a_ref, dtb_ref, out_ref, h_ref) = refs
    s = pl.program_id(1)
    c = _ssd_chunk_index(s, nctx=nctx, nlat=nlat, reverse=reverse)
    bq = n_sub * q

    @pl.when(s == 0)
    def _():
        h_ref[...] = jnp.zeros_like(h_ref)

    x = xbc_ref[...]
    rows = lax.broadcasted_iota(jnp.int32, x.shape, 0)
    has_prev = jnp.logical_and(c != 0, c != nctx).astype(F32)
    has_next = jnp.logical_and(c != nctx - 1, c != nctx + nlat - 1).astype(F32)
    xm1 = jnp.where(rows == 0, prev_ref[7:8, :] * has_prev, pltpu.roll(x, 1, 0))
    xp1 = jnp.where(rows == bq - 1, next_ref[0:1, :] * has_next, pltpu.roll(x, bq - 1, 0))
    conv = (convw_ref[0:1, :] * xm1 + convw_ref[1:2, :] * x + convw_ref[2:3, :] * xp1
            + convb_ref[...])
    act = _silu(conv)
    dt_all = _softplus(dt_ref[...] + dtb_ref[...])
    a_all = dt_all * a_ref[...]

    li = lax.broadcasted_iota(jnp.int32, (q, q), 0)
    si = lax.broadcasted_iota(jnp.int32, (q, q), 1)
    mask = (si >= li) if reverse else (si <= li)
    mask_t = ((li >= si) if reverse else (li <= si)).astype(BF16)
    edge = 0 if reverse else q - 1
    hh = lax.broadcasted_iota(jnp.int32, (HEAD_LANES, SSD_WIDTH), 0)
    ll = lax.broadcasted_iota(jnp.int32, (HEAD_LANES, SSD_WIDTH), 1)
    expand = (lax.shift_right_logical(ll, int(math.log2(SSD_HEAD_DIM))) == hh).astype(BF16)
    gw = SSD_WIDTH // SSD_GROUPS
    heads_per_group = SSD_HEADS // SSD_GROUPS
    lane = lax.broadcasted_iota(jnp.int32, (q, 2 * SSD_HEAD_DIM), 1)

    def chunk(act_c, dt, a, h_old):
        xs = act_c[:, :SSD_WIDTH]
        bm = act_c[:, SSD_WIDTH:SSD_WIDTH + SSD_GROUPS * SSD_STATE]
        cm = act_c[:, SSD_WIDTH + SSD_GROUPS * SSD_STATE:]
        cum_t = _dot_select(a.T, mask_t)
        cum = cum_t.T
        dt_e = _dot_select(dt, expand)
        cum_e = _dot_select(cum, expand)
        tot_e = cum_e[edge:edge + 1, :]
        xdt = xs * dt_e
        xdec = (xdt * jnp.exp(tot_e - cum_e)).astype(BF16)
        off_scale = jnp.exp(cum_e)
        y_groups = []
        st_groups = []
        for g in range(SSD_GROUPS):
            bg = bm[:, g * SSD_STATE:(g + 1) * SSD_STATE]
            cg = cm[:, g * SSD_STATE:(g + 1) * SSD_STATE].astype(BF16)
            cb = lax.dot_general(cg, bg.astype(BF16), (((1,), (1,)), ((), ())),
                                 preferred_element_type=F32)
            pairs = []
            for pr in range(heads_per_group // 2):
                lo = g * gw + pr * 2 * SSD_HEAD_DIM
                xp = xdt[:, lo:lo + 2 * SSD_HEAD_DIM]
                x_bd = jnp.concatenate([jnp.where(lane < SSD_HEAD_DIM, xp, 0.0),
                                        jnp.where(lane < SSD_HEAD_DIM, 0.0, xp)], axis=0)
                ms = []
                for k in range(2):
                    h = g * heads_per_group + pr * 2 + k
                    seg = cum[:, h:h + 1] - cum_t[h:h + 1, :]
                    lm = jnp.exp(jnp.where(mask, seg, NEG_INF))
                    ms.append((cb * lm).astype(BF16))
                pairs.append(jnp.dot(jnp.concatenate(ms, axis=1), x_bd.astype(BF16),
                                     preferred_element_type=F32))
            y_diag = jnp.concatenate(pairs, axis=1)
            hg = h_old[:, g * gw:(g + 1) * gw]
            y_off = jnp.dot(cg, hg.astype(BF16), preferred_element_type=F32)
            y_groups.append(y_diag + y_off * off_scale[:, g * gw:(g + 1) * gw])
            st_groups.append(jnp.dot(bg.T.astype(BF16), xdec[:, g * gw:(g + 1) * gw],
                                     preferred_element_type=F32))
        y = jnp.concatenate(y_groups, axis=1)
        return y, xs, h_old * jnp.exp(tot_e) + jnp.concatenate(st_groups, axis=1)

    h = h_ref[...]
    order = list(range(n_sub))
    if reverse:
        order.reverse()
    for sc in order:
        r = slice(sc * q, (sc + 1) * q)
        y, xs, h = chunk(act[r, :], dt_all[r, :], a_all[r, :], h)
        if final:
            y = y + yprev_ref[r, :] + xs * dskip_ref[...]
            gated = y * _silu(z_ref[r, :])
            ms = jnp.mean(gated * gated, axis=-1, keepdims=True)
            out_ref[r, :] = gated * lax.rsqrt(ms + EPS) * ng_ref[...]
        else:
            out_ref[r, :] = y
    h_ref[...] = h


def _ssd_pass(xbc, dt, z, yprev, convw, convb, a_row, dtb_row, dskip, ng, *,
              nb, ctx_len, seq, q, n_sub, direction):
    t = xbc.shape[0]
    bq = n_sub * q
    nctx, nlat = ctx_len // bq, seq // bq
    reverse = direction == 1
    final = direction == 1
    q8 = bq // 8
    last8 = t // 8 - 1

    def blk(b, s):
        c = _ssd_chunk_index(s, nctx=nctx, nlat=nlat, reverse=reverse)
        return _ssd_row_block(b, c, nb=nb, nctx=nctx, nlat=nlat)

    row_map = lambda b, s: (blk(b, s), 0)
    prev_map = lambda b, s: (jnp.maximum(blk(b, s) * q8 - 1, 0), 0)
    next_map = lambda b, s: (jnp.minimum((blk(b, s) + 1) * q8, last8), 0)
    dt_map = lambda b, s: (blk(b, s), direction)
    const = lambda b, s: (0, 0)

    in_specs = [pl.BlockSpec((bq, XBC_WIDTH), row_map),
                pl.BlockSpec((8, XBC_WIDTH), prev_map),
                pl.BlockSpec((8, XBC_WIDTH), next_map),
                pl.BlockSpec((bq, HEAD_LANES), dt_map)]
    args = [xbc, xbc, xbc, dt]
    if final:
        in_specs += [pl.BlockSpec((bq, SSD_WIDTH), row_map)] * 2
        args += [z, yprev]
    in_specs += [pl.BlockSpec(convw.shape, const), pl.BlockSpec(convb.shape, const),
                 pl.BlockSpec(a_row.shape, const), pl.BlockSpec(dtb_row.shape, const)]
    args += [convw, convb, a_row, dtb_row]
    if final:
        in_specs += [pl.BlockSpec(dskip.shape, const), pl.BlockSpec(ng.shape, const)]
        args += [dskip, ng]
    kern = functools.partial(_ssd_kernel, q=q, n_sub=n_sub, nctx=nctx, nlat=nlat,
                             reverse=reverse, final=final)
    return pl.pallas_call(
        kern,
        grid=(nb, nctx + nlat),
        in_specs=in_specs,
        out_specs=pl.BlockSpec((bq, SSD_WIDTH), row_map),
        out_shape=jax.ShapeDtypeStruct((t, SSD_WIDTH), F32),
        scratch_shapes=[pltpu.VMEM((SSD_STATE, SSD_WIDTH), F32)],
        compiler_params=_cparams(("arbitrary", "arbitrary")),
        name="ssd_bwd" if final else "ssd_fwd",
    )(*args)


def _s5_block_index(i, *, nctx, nblk, reverse):
    if not reverse:
        return i
    return jnp.where(i < nctx, nctx - 1 - i, nblk - 1 - (i - nctx))


def _s5_kernel(*refs, steps, nb, reverse, final):
    if final:
        (u_ref, yprev_ref, bfull_ref, abar_ref, cfull_ref, dskip_ref, gluw_ref, glub_ref,
         out_ref, state_ref, x_ref) = refs
    else:
        u_ref, bfull_ref, abar_ref, cfull_ref, out_ref, state_ref, x_ref = refs

    @pl.when(pl.program_id(0) == 0)
    def _():
        state_ref[...] = jnp.zeros_like(state_ref)

    n = abar_ref.shape[1] // 2
    half = n // 2
    sub = S5_SUB_STEPS
    order = list(range(steps // sub))
    if reverse:
        order.reverse()

    def rows(s):
        return slice(s * sub * nb, (s + 1) * sub * nb)

    def project_in(s):
        x_ref[rows(s), :] = jnp.dot(u_ref[rows(s), :].astype(BF16), bfull_ref[...],
                                    preferred_element_type=F32)

    def scan(s):
        for part in range(2):
            re = slice(part * half, (part + 1) * half)
            im = slice(n + part * half, n + (part + 1) * half)
            ar = abar_ref[:, re]
            ai = abar_ref[:, im]
            xr = state_ref[:, re]
            xi = state_ref[:, im]
            for k in range(sub):
                kk = sub - 1 - k if reverse else k
                r = slice((s * sub + kk) * nb, (s * sub + kk + 1) * nb)
                xr, xi = (ar * xr - ai * xi + x_ref[r, re], ar * xi + ai * xr + x_ref[r, im])
                x_ref[r, re] = xr
                x_ref[r, im] = xi
            state_ref[:, re] = xr
            state_ref[:, im] = xi

    def read_out(s):
        y = jnp.dot(x_ref[rows(s), :].astype(BF16), cfull_ref[...],
                    preferred_element_type=F32)
        if final:
            y = jax.nn.gelu(y + yprev_ref[rows(s), :] + u_ref[rows(s), :] * dskip_ref[...])
            gate = jnp.dot(y.astype(BF16), gluw_ref[...], preferred_element_type=F32)
            y = y * jax.nn.sigmoid(gate + glub_ref[...])
        out_ref[rows(s), :] = y

    project_in(order[0])
    for idx, s in enumerate(order):
        if idx + 1 < len(order):
            project_in(order[idx + 1])
        if idx > 0:
            read_out(order[idx - 1])
        scan(s)
    read_out(order[-1])


def _s5_pass(u_scan, yprev, bfull, abar, cfull, dskip, gluw, glub, *,
             nb, ctx_len, steps, direction):
    rows, width = u_scan.shape
    nblk = rows // (steps * nb)
    nctx = ctx_len // steps
    reverse = direction == 1
    final = direction == 1
    row_map = lambda i: (_s5_block_index(i, nctx=nctx, nblk=nblk, reverse=reverse), 0)
    const = lambda i: (0, 0)
    blk = pl.BlockSpec((steps * nb, width), row_map)
    in_specs = [blk]
    args = [u_scan]
    if final:
        in_specs.append(blk)
        args.append(yprev)
    in_specs += [pl.BlockSpec(bfull.shape, const), pl.BlockSpec(abar.shape, const),
                 pl.BlockSpec(cfull.shape, const)]
    args += [bfull, abar, cfull]
    if final:
        in_specs += [pl.BlockSpec(dskip.shape, const), pl.BlockSpec(gluw.shape, const),
                     pl.BlockSpec(glub.shape, const)]
        args += [dskip, gluw, glub]
    kern = functools.partial(_s5_kernel, steps=steps, nb=nb, reverse=reverse, final=final)
    return pl.pallas_call(
        kern,
        grid=(nblk,),
        in_specs=in_specs,
        out_specs=blk,
        out_shape=jax.ShapeDtypeStruct((rows, width), F32),
        scratch_shapes=[pltpu.VMEM(abar.shape, F32),
                        pltpu.VMEM((steps * nb, abar.shape[1]), F32)],
        compiler_params=_cparams(("arbitrary",)),
        name="s5_bwd" if final else "s5_fwd",
    )(*args)


def _s5_params(lam_re, lam_im, log_step, b_re, b_im, c_re, c_im, nb):
    g, p = lam_re.shape[1], lam_re.shape[2]
    step = jnp.exp(log_step)[..., None]
    mag = jnp.exp(lam_re * step)
    ar = mag * jnp.cos(lam_im * step)
    ai = mag * jnp.sin(lam_im * step)
    den = lam_re * lam_re + lam_im * lam_im
    kr = ((ar - 1.0) * lam_re + ai * lam_im) / den
    ki = (ai * lam_re - (ar - 1.0) * lam_im) / den
    bbr = kr[..., None] * b_re - ki[..., None] * b_im
    bbi = kr[..., None] * b_im + ki[..., None] * b_re
    eye = jnp.eye(g, dtype=F32)
    cdim = b_re.shape[-1]

    def bmat(bb):
        return jnp.einsum('gpc,gh->gchp', bb, eye).reshape(g * cdim, g * p)

    bfull = jnp.stack([jnp.concatenate([bmat(bbr[d]), bmat(bbi[d])], axis=1)
                       for d in range(2)]).astype(BF16)
    abar = jnp.stack([jnp.broadcast_to(
        jnp.concatenate([ar[d].reshape(-1), ai[d].reshape(-1)])[None, :], (nb, 2 * g * p))
        for d in range(2)])

    def cmat(cc):
        return jnp.einsum('gcp,gh->gphc', cc, eye).reshape(g * p, g * cdim)

    cfull = jnp.concatenate([cmat(c_re), -cmat(c_im)], axis=0).astype(BF16)
    return bfull, abar, cfull


def _out_proj_kernel(x_ref, ssd_ref, s5_ref, mod_ref, w1_ref, w2_ref, g_ref, b_ref, o_ref, *,
                     d, alpha, n_ctx_tiles, lat_tiles, ctx_row):
    row = _mod_row(pl.program_id(0), n_ctx_tiles, lat_tiles, ctx_row)
    m2 = mod_ref[pl.ds(row, 1), 2 * d:3 * d]
    mix = (jnp.dot(ssd_ref[...].astype(BF16), w1_ref[...], preferred_element_type=F32)
           + jnp.dot(s5_ref[...].astype(BF16), w2_ref[...], preferred_element_type=F32))
    o_ref[...] = _layer_norm(alpha * x_ref[...] + m2 * mix, g_ref[...], b_ref[...])


def _out_proj(x, ssd, s5, mod, w1, w2, g, b, *, tm, alpha, n_ctx_tiles, lat_tiles, ctx_row):
    t, d = x.shape
    kern = functools.partial(_out_proj_kernel, d=d, alpha=alpha, n_ctx_tiles=n_ctx_tiles,
                             lat_tiles=lat_tiles, ctx_row=ctx_row)
    const = lambda i: (0, 0)
    return pl.pallas_call(
        kern,
        grid=(t // tm,),
        in_specs=[pl.BlockSpec((tm, d), lambda i: (i, 0)),
                  pl.BlockSpec((tm, ssd.shape[1]), lambda i: (i, 0)),
                  pl.BlockSpec((tm, s5.shape[1]), lambda i: (i, 0)),
                  pl.BlockSpec(mod.shape, const), pl.BlockSpec(w1.shape, const),
                  pl.BlockSpec(w2.shape, const), pl.BlockSpec(g.shape, const),
                  pl.BlockSpec(b.shape, const)],
        out_specs=pl.BlockSpec((tm, d), lambda i: (i, 0)),
        out_shape=jax.ShapeDtypeStruct((t, d), F32),
        compiler_params=_cparams(("arbitrary",)),
        name="out_proj_ln",
    )(x, ssd, s5, mod, w1, w2, g, b)


def _peer_kernel(x_ref, mod_ref, wqt_ref, keys_ref, u_ref, vt_ref, g_ref, b_ref, o_ref,
                 hbt_ref, acc_ref, q_ref, top_ref, cand_ref, ord_ref, work_ref, s0_ref, s1_ref,
                 rnk_ref, rank_ref, e1_ref, n_ref, r_ref, *,
                 d, nk, heads, half, alpha, n_ctx_tiles, lat_tiles, ctx_row):
    i = pl.program_id(0)
    j = pl.program_id(1)
    tm = x_ref.shape[0]
    eb = u_ref.shape[0]
    nsel = PEER_TOPK + 1
    pad_rows = top_ref.shape[1]
    pk = 16

    @pl.when(j == 0)
    def _prepare():
        row = _mod_row(i, n_ctx_tiles, lat_tiles, ctx_row)
        m3 = mod_ref[pl.ds(row, 1), 3 * d:4 * d]
        m4 = mod_ref[pl.ds(row, 1), 4 * d:5 * d]
        h = x_ref[...] * (1.0 + m4) + m3
        hbt = h.T.astype(BF16)
        hbt_ref[...] = hbt
        q_ref[...] = jnp.dot(wqt_ref[...], hbt, preferred_element_type=F32)
        acc_ref[...] = jnp.zeros_like(acc_ref)

        def per_head(hd, carry):
            top_ref[...] = jnp.full(top_ref.shape, NEG_INF, F32)
            rnk_ref[...] = jnp.full(rnk_ref.shape, 2.0 * nsel, F32)
            for c, dst in ((0, s0_ref), (1, s1_ref)):
                off = pl.multiple_of((hd * 2 + c) * nk, nk)
                qoff = pl.multiple_of((hd * 2 + c) * half, half)
                sc = jnp.dot(keys_ref[pl.ds(off, nk), :], q_ref[pl.ds(qoff, half), :],
                             precision=HIGHEST, preferred_element_type=F32)
                dst[...] = sc
                work_ref[c] = sc

            def take(r, carry):
                for c in range(2):
                    cur = work_ref[c]
                    m = jnp.max(cur, axis=0, keepdims=True)
                    top_ref[c, pl.ds(r, 1), :] = m
                    hit = cur >= m
                    work_ref[c] = jnp.where(hit, NEG_INF, cur)
                    if c == 1:
                        rnk_ref[...] = jnp.where(hit, lax.convert_element_type(r + 1, F32),
                                                 rnk_ref[...])
                return carry

            lax.fori_loop(0, nsel, take, 0)
            v0 = top_ref[0]
            v1 = top_ref[1]
            cands = [v0[0:1, :] + v1]
            cands += [v0[a:a + 1, :] + v1[0:8, :] for a in range(1, 8)]
            cands += [v0[8:pad_rows, :] + v1[0:1, :]]
            cand = jnp.concatenate(cands, axis=0)
            cand_ref[...] = cand

            def take2(r, carry):
                cur = cand_ref[...]
                m = jnp.max(cur, axis=0, keepdims=True)
                ord_ref[pl.ds(r, 1), :] = m
                cand_ref[...] = jnp.where(cur >= m, NEG_INF, cur)
                return carry

            lax.fori_loop(0, nsel, take2, 0)
            tau = 0.5 * (ord_ref[nsel - 2:nsel - 1, :] + ord_ref[nsel - 1:nsel, :])
            top = v0[0:1, :] + v1[0:1, :]
            z = jnp.sum(jnp.where(cand > tau, jnp.exp(cand - top), 0.0), axis=0, keepdims=True)
            s0 = s0_ref[...]
            theta = tau - s0
            count = jnp.zeros_like(theta)
            for bb in range(PEER_TOPK):
                count = count + (v1[bb:bb + 1, :] >= theta).astype(F32)
            n_ref[hd] = count
            r_ref[hd] = jnp.exp(s0 - v0[0:1, :]) / z
            e1 = jnp.exp(s1_ref[...] - v1[0:1, :])
            e1_ref[hd] = e1.astype(BF16).reshape(nk // pk, pk, tm)
            rank_ref[hd] = rnk_ref[...].astype(BF16).reshape(nk // pk, pk, tm)
            return carry

        lax.fori_loop(0, heads, per_head, 0)

    lanes = 128
    keys_per_step = eb // nk
    sub_keys = 4
    base = pl.multiple_of(j * keys_per_step, keys_per_step)
    hbt = hbt_ref[...]
    sub_rows = sub_keys * nk
    n_sub = keys_per_step // sub_keys

    def expert_acts(sub):
        return jnp.dot(u_ref[sub * sub_rows:(sub + 1) * sub_rows, :], hbt,
                       preferred_element_type=F32)

    def project(sub, gate):
        acc_ref[...] += jnp.dot(vt_ref[:, sub * sub_rows:(sub + 1) * sub_rows], gate,
                                preferred_element_type=F32)

    a_next = expert_acts(0)
    gate_prev = None
    for sub in range(n_sub):
        a_sub = a_next
        if sub + 1 < n_sub:
            a_next = expert_acts(sub + 1)
        if gate_prev is not None:
            project(sub - 1, gate_prev)
        rows = []
        for k2 in range(sub_keys):
            ii = sub * sub_keys + k2
            cols = []
            for lt in range(tm // lanes):
                ls = slice(lt * lanes, (lt + 1) * lanes)
                w = jnp.zeros((nk // pk, pk, lanes), BF16)
                for hd in range(heads):
                    cnt = n_ref[hd, pl.ds(base, keys_per_step), ls][ii:ii + 1, :]
                    rr = r_ref[hd, pl.ds(base, keys_per_step), ls][ii:ii + 1, :]
                    cnt = jnp.broadcast_to(cnt, (pk, lanes)).astype(BF16)[None]
                    rr = jnp.broadcast_to(rr, (pk, lanes)).astype(BF16)[None]
                    w = w + jnp.where(rank_ref[hd, :, :, ls] <= cnt, e1_ref[hd, :, :, ls],
                                      jnp.zeros((), BF16)) * rr
                a = a_sub[k2 * nk:(k2 + 1) * nk, ls].astype(BF16).reshape(nk // pk, pk, lanes)
                cols.append((w * jax.nn.gelu(a)).reshape(nk, lanes))
            rows.append(jnp.concatenate(cols, axis=1))
        gate_prev = jnp.concatenate(rows, axis=0)
    project(n_sub - 1, gate_prev)

    @pl.when(j == pl.num_programs(1) - 1)
    def _finish():
        row = _mod_row(i, n_ctx_tiles, lat_tiles, ctx_row)
        m5 = mod_ref[pl.ds(row, 1), 5 * d:6 * d]
        f = acc_ref[...].T
        o_ref[...] = _layer_norm(alpha * x_ref[...] + m5 * f, g_ref[...], b_ref[...])


def _peer(x, mod, wqt, keys2d, u_b, vt_b, g, b, *, layer, tm, heads, alpha,
          skip_tiles, n_ctx_tiles, lat_tiles, ctx_row):
    t, d = x.shape
    _, nblk, _, eb = vt_b.shape
    nk = keys2d.shape[0] // (2 * heads)
    half = keys2d.shape[1]
    pad_rows = 24
    ncand = pad_rows + 7 * 8 + (pad_rows - 8)
    kern = functools.partial(_peer_kernel, d=d, nk=nk, heads=heads, half=half, alpha=alpha,
                             n_ctx_tiles=n_ctx_tiles, lat_tiles=lat_tiles, ctx_row=ctx_row)
    const = lambda i, j: (0, 0)
    return pl.pallas_call(
        kern,
        grid=(t // tm - skip_tiles, nblk),
        in_specs=[pl.BlockSpec((tm, d), lambda i, j: (i + skip_tiles, 0)),
                  pl.BlockSpec(mod.shape, const), pl.BlockSpec(wqt.shape, const),
                  pl.BlockSpec(keys2d.shape, const),
                  pl.BlockSpec((None, eb, d), lambda i, j: (layer, j, 0)),
                  pl.BlockSpec((None, None, d, eb), lambda i, j: (layer, j, 0, 0)),
                  pl.BlockSpec(g.shape, const), pl.BlockSpec(b.shape, const)],
        out_specs=pl.BlockSpec((tm, d), lambda i, j: (i, 0)),
        out_shape=jax.ShapeDtypeStruct((t - skip_tiles * tm, d), F32),
        scratch_shapes=[pltpu.VMEM((d, tm), BF16),
                        pltpu.VMEM((d, tm), F32),
                        pltpu.VMEM((wqt.shape[0], tm), F32),
                        pltpu.VMEM((2, pad_rows, tm), F32),
                        pltpu.VMEM((ncand, tm), F32),
                        pltpu.VMEM((pad_rows, tm), F32),
                        pltpu.VMEM((2, nk, tm), F32),
                        pltpu.VMEM((nk, tm), F32),
                        pltpu.VMEM((nk, tm), F32),
                        pltpu.VMEM((nk, tm), F32),
                        pltpu.VMEM((heads, nk // 16, 16, tm), BF16),
                        pltpu.VMEM((heads, nk // 16, 16, tm), BF16),
                        pltpu.VMEM((heads, nk, tm), F32),
                        pltpu.VMEM((heads, nk, tm), F32)],
        compiler_params=_cparams(("arbitrary", "arbitrary")),
        name="peer_ln",
    )(x, mod, wqt, keys2d, u_b, vt_b, g, b)


def _pick_tile(limit, *sizes):
    tm = limit
    while any(s % tm for s in sizes):
        tm //= 2
    return tm


def kernel(x, c, ctx, c_ctx, w_mod, b_mod, w_in, conv_w, conv_b, ssd_a_log, ssd_dt_bias, ssd_d, ssd_norm_g, s5_lam_re, s5_lam_im, s5_log_step, s5_b_re, s5_b_im, s5_c_re, s5_c_im, s5_d, s5_glu_w, s5_glu_b, w_out, ln1_g, ln1_b, peer_wq, peer_keys, peer_u, peer_v, ln2_g, ln2_b):
    nb, seq, d = x.shape
    ctx_len = ctx.shape[1]
    depth = w_mod.shape[0]
    s5_width = s5_d.shape[1]
    heads = peer_keys.shape[1]
    nk = peer_keys.shape[3]
    alpha = (2.0 * depth) ** 0.25
    rows_lat = seq // GRID_W

    tm = _pick_tile(512, nb * ctx_len, seq)
    tp = _pick_tile(512, nb * ctx_len, seq)
    q = _pick_tile(128, ctx_len, seq)
    ssd_sub = 2 if ctx_len % (2 * q) == 0 and seq % (2 * q) == 0 else 1
    steps = _pick_tile(256, ctx_len, seq)
    eb = 16 * nk
    ctx_row = nb

    mod_rows = -(-(nb + 1) // 8) * 8
    cc = jnp.concatenate([c, c_ctx[None, :], jnp.zeros((mod_rows - nb - 1, d), F32)], axis=0)
    mods = _mod_table(cc, w_mod, b_mod)

    xt = jnp.concatenate([ctx.reshape(nb * ctx_len, d), x.reshape(nb * seq, d)], axis=0)

    def pad_lanes(w, n):
        return jnp.pad(w, ((0, 0), (0, n - w.shape[1])))

    u_all = peer_u.astype(BF16)
    vt_all = peer_v.astype(BF16).reshape(depth, nk * nk // eb, eb, d).transpose(0, 1, 3, 2)

    for i in range(depth):
        wi = w_in[i]
        o_dt = SSD_WIDTH + XBC_WIDTH
        w_pad = jnp.concatenate(
            [wi[:, :o_dt],
             pad_lanes(wi[:, o_dt:o_dt + SSD_HEADS], HEAD_LANES),
             pad_lanes(wi[:, o_dt + SSD_HEADS:o_dt + 2 * SSD_HEADS], HEAD_LANES),
             wi[:, o_dt + 2 * SSD_HEADS:]], axis=1).astype(BF16)
        a_rows = pad_lanes(-jnp.exp(ssd_a_log[i]), HEAD_LANES)
        dtb_rows = pad_lanes(ssd_dt_bias[i], HEAD_LANES)
        dskip_ssd = jnp.repeat(ssd_d[i], SSD_HEAD_DIM)[None, :]
        bfull, abar, cfull = _s5_params(s5_lam_re[i], s5_lam_im[i], s5_log_step[i],
                                        s5_b_re[i], s5_b_im[i], s5_c_re[i], s5_c_im[i], nb)

        z, xbc, dt, u = _in_proj(xt, mods[i], w_pad, tm=tm, n_ctx_tiles=nb * ctx_len // tm,
                                 lat_tiles=seq // tm, ctx_row=ctx_row, s5_width=s5_width)
        ssd_args = dict(nb=nb, ctx_len=ctx_len, seq=seq, q=q, n_sub=ssd_sub)
        y_f = _ssd_pass(xbc, dt, None, None, conv_w[i], conv_b[i][None, :], a_rows[0:1],
                        dtb_rows[0:1], None, None, direction=0, **ssd_args)
        ssd_out = _ssd_pass(xbc, dt, z, y_f, conv_w[i], conv_b[i][None, :], a_rows[1:2],
                            dtb_rows[1:2], dskip_ssd, ssd_norm_g[i][None, :], direction=1,
                            **ssd_args)

        u_ctx = u[:nb * ctx_len].reshape(nb, ctx_len, s5_width).transpose(1, 0, 2)
        u_lat = u[nb * ctx_len:].reshape(nb, rows_lat, GRID_W, s5_width).transpose(2, 1, 0, 3)
        u_scan = jnp.concatenate([u_ctx.reshape(ctx_len * nb, s5_width),
                                  u_lat.reshape(seq * nb, s5_width)], axis=0)
        s5_args = dict(nb=nb, ctx_len=ctx_len, steps=steps)
        ys_f = _s5_pass(u_scan, None, bfull[0], abar[0], cfull, None, None, None,
                        direction=0, **s5_args)
        s5_scan = _s5_pass(u_scan, ys_f, bfull[1], abar[1], cfull, s5_d[i][None, :],
                           s5_glu_w[i].astype(BF16), s5_glu_b[i][None, :], direction=1,
                           **s5_args)
        s5_ctx = s5_scan[:ctx_len * nb].reshape(ctx_len, nb, s5_width).transpose(1, 0, 2)
        s5_lat = s5_scan[ctx_len * nb:].reshape(GRID_W, rows_lat, nb, s5_width)
        s5_lat = s5_lat.transpose(2, 1, 0, 3)
        s5_out = jnp.concatenate([s5_ctx.reshape(nb * ctx_len, s5_width),
                                  s5_lat.reshape(nb * seq, s5_width)], axis=0)

        wo = w_out[i].astype(BF16)
        xt = _out_proj(xt, ssd_out, s5_out, mods[i], wo[:SSD_WIDTH], wo[SSD_WIDTH:],
                       ln1_g[i][None, :], ln1_b[i][None, :], tm=tm, alpha=alpha,
                       n_ctx_tiles=nb * ctx_len // tm, lat_tiles=seq // tm, ctx_row=ctx_row)

        last = i == depth - 1
        ctx_tiles = nb * ctx_len // tp
        xt = _peer(xt, mods[i], peer_wq[i].T.astype(BF16),
                   peer_keys[i].reshape(heads * 2 * nk, -1), u_all, vt_all,
                   ln2_g[i][None, :], ln2_b[i][None, :], layer=i, tm=tp, heads=heads, alpha=alpha,
                   skip_tiles=ctx_tiles if last else 0,
                   n_ctx_tiles=0 if last else ctx_tiles, lat_tiles=seq // tp, ctx_row=ctx_row)

    return xt.reshape(nb, seq, d)
```

```python
import functools
import math

import jax
import jax.numpy as jnp
from jax import lax
from jax.experimental import pallas as pl
from jax.experimental.pallas import tpu as pltpu

GRID_W = 64

SSD_HEADS = 12
SSD_HEAD_DIM = 64
SSD_GROUPS = 2
SSD_STATE = 128
SSD_WIDTH = SSD_HEADS * SSD_HEAD_DIM
XBC_WIDTH = SSD_WIDTH + 2 * SSD_GROUPS * SSD_STATE
HEAD_LANES = 128
S5_GROUP = 16
S5_STATE = 64
S5_SUB_STEPS = 64
PEER_TOPK = 16
EPS = 1e-5

F32 = jnp.float32
BF16 = jnp.bfloat16
HIGHEST = lax.Precision.HIGHEST
NEG_INF = float("-inf")

VMEM_LIMIT = 56 * 1024 * 1024


def _cparams(sem):
    return pltpu.CompilerParams(dimension_semantics=sem, vmem_limit_bytes=VMEM_LIMIT)


def _silu(x):
    return x * jax.nn.sigmoid(x)


def _softplus(x):
    return jnp.maximum(x, 0.0) + jnp.log1p(jnp.exp(-jnp.abs(x)))


def _dot_select(x, sel):
    hi = x.astype(BF16)
    r1 = x - hi.astype(F32)
    mid = r1.astype(BF16)
    lo = (r1 - mid.astype(F32)).astype(BF16)
    return (jnp.dot(hi, sel, preferred_element_type=F32)
            + jnp.dot(mid, sel, preferred_element_type=F32)
            + jnp.dot(lo, sel, preferred_element_type=F32))


def _layer_norm(y, g, b):
    mu = jnp.mean(y, axis=-1, keepdims=True)
    d = y - mu
    var = jnp.mean(d * d, axis=-1, keepdims=True)
    return d * lax.rsqrt(var + EPS) * g + b


def _mod_row(tile, n_ctx_tiles, lat_tiles_per_batch, ctx_row):
    return jnp.where(tile < n_ctx_tiles, ctx_row,
                     (tile - n_ctx_tiles) // lat_tiles_per_batch)


def _mod_kernel(c_ref, w_ref, b_ref, o_ref):
    o_ref[...] = jnp.dot(_silu(c_ref[...]), w_ref[...], precision=HIGHEST,
                         preferred_element_type=F32) + b_ref[...]


def _mod_table(cc, w_mod, b_mod):
    depth, d, d6 = w_mod.shape
    rows = cc.shape[0]
    nj = d6 // d
    return pl.pallas_call(
        _mod_kernel,
        grid=(depth, nj),
        in_specs=[pl.BlockSpec((rows, d), lambda l, j: (0, 0)),
                  pl.BlockSpec((None, d, d), lambda l, j: (l, 0, j)),
                  pl.BlockSpec((None, 1, d), lambda l, j: (l, 0, j))],
        out_specs=pl.BlockSpec((None, rows, d), lambda l, j: (l, 0, j)),
        out_shape=jax.ShapeDtypeStruct((depth, rows, d6), F32),
        compiler_params=_cparams(("arbitrary", "arbitrary")),
        name="mod_table",
    )(cc, w_mod, b_mod.reshape(depth, 1, d6))


def _in_proj_kernel(x_ref, mod_ref, w_ref, z_ref, xbc_ref, dt_ref, u_ref, *,
                    d, n_ctx_tiles, lat_tiles, ctx_row):
    row = _mod_row(pl.program_id(0), n_ctx_tiles, lat_tiles, ctx_row)
    m0 = mod_ref[pl.ds(row, 1), 0:d]
    m1 = mod_ref[pl.ds(row, 1), d:2 * d]
    xm = (x_ref[...] * (1.0 + m1) + m0).astype(BF16)
    p = jnp.dot(xm, w_ref[...], preferred_element_type=F32)
    o = 0
    for ref in (z_ref, xbc_ref, dt_ref, u_ref):
        w = ref.shape[-1]
        ref[...] = p[:, o:o + w]
        o += w


def _in_proj(x, mod, w, *, tm, n_ctx_tiles, lat_tiles, ctx_row, s5_width):
    t, d = x.shape
    widths = (SSD_WIDTH, XBC_WIDTH, 2 * HEAD_LANES, s5_width)
    kern = functools.partial(_in_proj_kernel, d=d, n_ctx_tiles=n_ctx_tiles,
                             lat_tiles=lat_tiles, ctx_row=ctx_row)
    return pl.pallas_call(
        kern,
        grid=(t // tm,),
        in_specs=[pl.BlockSpec((tm, d), lambda i: (i, 0)),
                  pl.BlockSpec(mod.shape, lambda i: (0, 0)),
                  pl.BlockSpec(w.shape, lambda i: (0, 0))],
        out_specs=[pl.BlockSpec((tm, wd), lambda i: (i, 0)) for wd in widths],
        out_shape=[jax.ShapeDtypeStruct((t, wd), F32) for wd in widths],
        compiler_params=_cparams(("arbitrary",)),
        name="in_proj",
    )(x, mod, w)


def _ssd_chunk_index(s, *, nctx, nlat, reverse):
    if not reverse:
        return s
    return jnp.where(s < nctx, nctx - 1 - s, nctx + nlat - 1 - (s - nctx))


def _ssd_row_block(b, c, *, nb, nctx, nlat):
    return jnp.where(c < nctx, b * nctx + c, nb * nctx + b * nlat + (c - nctx))


def _ssd_kernel(*refs, q, n_sub, nctx, nlat, reverse, final):
    if final:
        (xbc_ref, prev_ref, next_ref, dt_ref, z_ref, yprev_ref, convw_ref, convb_ref,
         a_ref, dtb_ref, dskip_ref, ng_ref, out_ref, h_ref) = refs
    else:
        (xbc_ref, prev_ref, next_ref, dt_ref, convw_ref, convb_ref,
         a_ref, dtb_ref, out_ref, h_ref) = refs
    s = pl.program_id(1)
    c = _ssd_chunk_index(s, nctx=nctx, nlat=nlat, reverse=reverse)
    bq = n_sub * q

    @pl.when(s == 0)
    def _():
        h_ref[...] = jnp.zeros_like(h_ref)

    x = xbc_ref[...]
    rows = lax.broadcasted_iota(jnp.int32, x.shape, 0)
    has_prev = jnp.logical_and(c != 0, c != nctx).astype(F32)
    has_next = jnp.logical_and(c != nctx - 1, c != nctx + nlat - 1).astype(F32)
    xm1 = jnp.where(rows == 0, prev_ref[7:8, :] * has_prev, pltpu.roll(x, 1, 0))
    xp1 = jnp.where(rows == bq - 1, next_ref[0:1, :] * has_next, pltpu.roll(x, bq - 1, 0))
    conv = (convw_ref[0:1, :] * xm1 + convw_ref[1:2, :] * x + convw_ref[2:3, :] * xp1
            + convb_ref[...])
    act = _silu(conv)
    dt_all = _softplus(dt_ref[...] + dtb_ref[...])
    a_all = dt_all * a_ref[...]

    li = lax.broadcasted_iota(jnp.int32, (q, q), 0)
    si = lax.broadcasted_iota(jnp.int32, (q, q), 1)
    mask = (si >= li) if reverse else (si <= li)
    mask_t = ((li >= si) if reverse else (li <= si)).astype(BF16)
    edge = 0 if reverse else q - 1
    hh = lax.broadcasted_iota(jnp.int32, (HEAD_LANES, SSD_WIDTH), 0)
    ll = lax.broadcasted_iota(jnp.int32, (HEAD_LANES, SSD_WIDTH), 1)
    expand = (lax.shift_right_logical(ll, int(math.log2(SSD_HEAD_DIM))) == hh).astype(BF16)
    gw = SSD_WIDTH // SSD_GROUPS
    heads_per_group = SSD_HEADS // SSD_GROUPS
    lane = lax.broadcasted_iota(jnp.int32, (q, 2 * SSD_HEAD_DIM), 1)

    def chunk(act_c, dt, a, h_old):
        xs = act_c[:, :SSD_WIDTH]
        bm = act_c[:, SSD_WIDTH:SSD_WIDTH + SSD_GROUPS * SSD_STATE]
        cm = act_c[:, SSD_WIDTH + SSD_GROUPS * SSD_STATE:]
        cum_t = _dot_select(a.T, mask_t)
        cum = cum_t.T
        dt_e = _dot_select(dt, expand)
        cum_e = _dot_select(cum, expand)
        tot_e = cum_e[edge:edge + 1, :]
        xdt = xs * dt_e
        xdec = (xdt * jnp.exp(tot_e - cum_e)).astype(BF16)
        off_scale = jnp.exp(cum_e)
        y_groups = []
        st_groups = []
        for g in range(SSD_GROUPS):
            bg = bm[:, g * SSD_STATE:(g + 1) * SSD_STATE]
            cg = cm[:, g * SSD_STATE:(g + 1) * SSD_STATE].astype(BF16)
            cb = lax.dot_general(cg, bg.astype(BF16), (((1,), (1,)), ((), ())),
                                 preferred_element_type=F32)
            pairs = []
            for pr in range(heads_per_group // 2):
                lo = g * gw + pr * 2 * SSD_HEAD_DIM
                xp = xdt[:, lo:lo + 2 * SSD_HEAD_DIM]
                x_bd = jnp.concatenate([jnp.where(lane < SSD_HEAD_DIM, xp, 0.0),
                                        jnp.where(lane < SSD_HEAD_DIM, 0.0, xp)], axis=0)
                ms = []
                for k in range(2):
                    h = g * heads_per_group + pr * 2 + k
                    seg = cum[:, h:h + 1] - cum_t[h:h + 1, :]
                    lm = jnp.exp(jnp.where(mask, seg, NEG_INF))
                    ms.append((cb * lm).astype(BF16))
                pairs.append(jnp.dot(jnp.concatenate(ms, axis=1), x_bd.astype(BF16),
                                     preferred_element_type=F32))
            y_diag = jnp.concatenate(pairs, axis=1)
            hg = h_old[:, g * gw:(g + 1) * gw]
            y_off = jnp.dot(cg, hg.astype(BF16), preferred_element_type=F32)
            y_groups.append(y_diag + y_off * off_scale[:, g * gw:(g + 1) * gw])
            st_groups.append(jnp.dot(bg.T.astype(BF16), xdec[:, g * gw:(g + 1) * gw],
                                     preferred_element_type=F32))
        y = jnp.concatenate(y_groups, axis=1)
        return y, xs, h_old * jnp.exp(tot_e) + jnp.concatenate(st_groups, axis=1)

    h = h_ref[...]
    order = list(range(n_sub))
    if reverse:
        order.reverse()
    for sc in order:
        r = slice(sc * q, (sc + 1) * q)
        y, xs, h = chunk(act[r, :], dt_all[r, :], a_all[r, :], h)
        if final:
            y = y + yprev_ref[r, :] + xs * dskip_ref[...]
            gated = y * _silu(z_ref[r, :])
            ms = jnp.mean(gated * gated, axis=-1, keepdims=True)
            out_ref[r, :] = gated * lax.rsqrt(ms + EPS) * ng_ref[...]
        else:
            out_ref[r, :] = y
    h_ref[...] = h


def _ssd_pass(xbc, dt, z, yprev, convw, convb, a_row, dtb_row, dskip, ng, *,
              nb, ctx_len, seq, q, n_sub, direction):
    t = xbc.shape[0]
    bq = n_sub * q
    nctx, nlat = ctx_len // bq, seq // bq
    reverse = direction == 1
    final = direction == 1
    q8 = bq // 8
    last8 = t // 8 - 1

    def blk(b, s):
        c = _ssd_chunk_index(s, nctx=nctx, nlat=nlat, reverse=reverse)
        return _ssd_row_block(b, c, nb=nb, nctx=nctx, nlat=nlat)

    row_map = lambda b, s: (blk(b, s), 0)
    prev_map = lambda b, s: (jnp.maximum(blk(b, s) * q8 - 1, 0), 0)
    next_map = lambda b, s: (jnp.minimum((blk(b, s) + 1) * q8, last8), 0)
    dt_map = lambda b, s: (blk(b, s), direction)
    const = lambda b, s: (0, 0)

    in_specs = [pl.BlockSpec((bq, XBC_WIDTH), row_map),
                pl.BlockSpec((8, XBC_WIDTH), prev_map),
                pl.BlockSpec((8, XBC_WIDTH), next_map),
                pl.BlockSpec((bq, HEAD_LANES), dt_map)]
    args = [xbc, xbc, xbc, dt]
    if final:
        in_specs += [pl.BlockSpec((bq, SSD_WIDTH), row_map)] * 2
        args += [z, yprev]
    in_specs += [pl.BlockSpec(convw.shape, const), pl.BlockSpec(convb.shape, const),
                 pl.BlockSpec(a_row.shape, const), pl.BlockSpec(dtb_row.shape, const)]
    args += [convw, convb, a_row, dtb_row]
    if final:
        in_specs += [pl.BlockSpec(dskip.shape, const), pl.BlockSpec(ng.shape, const)]
        args += [dskip, ng]
    kern = functools.partial(_ssd_kernel, q=q, n_sub=n_sub, nctx=nctx, nlat=nlat,
                             reverse=reverse, final=final)
    return pl.pallas_call(
        kern,
        grid=(nb, nctx + nlat),
        in_specs=in_specs,
        out_specs=pl.BlockSpec((bq, SSD_WIDTH), row_map),
        out_shape=jax.ShapeDtypeStruct((t, SSD_WIDTH), F32),
        scratch_shapes=[pltpu.VMEM((SSD_STATE, SSD_WIDTH), F32)],
        compiler_params=_cparams(("arbitrary", "arbitrary")),
        name="ssd_bwd" if final else "ssd_fwd",
    )(*args)


def _s5_block_index(i, *, nctx, nblk, reverse):
    if not reverse:
        return i
    return jnp.where(i < nctx, nctx - 1 - i, nblk - 1 - (i - nctx))


def _s5_kernel(*refs, steps, nb, reverse, final):
    if final:
        (u_ref, yprev_ref, bfull_ref, abar_ref, cfull_ref, dskip_ref, gluw_ref, glub_ref,
         out_ref, state_ref, x_ref) = refs
    else:
        u_ref, bfull_ref, abar_ref, cfull_ref, out_ref, state_ref, x_ref = refs

    @pl.when(pl.program_id(0) == 0)
    def _():
        state_ref[...] = jnp.zeros_like(state_ref)

    n = abar_ref.shape[1] // 2
    half = n // 2
    sub = S5_SUB_STEPS
    order = list(range(steps // sub))
    if reverse:
        order.reverse()

    def rows(s):
        return slice(s * sub * nb, (s + 1) * sub * nb)

    def project_in(s):
        x_ref[rows(s), :] = jnp.dot(u_ref[rows(s), :].astype(BF16), bfull_ref[...],
                                    preferred_element_type=F32)

    def scan(s):
        for part in range(2):
            re = slice(part * half, (part + 1) * half)
            im = slice(n + part * half, n + (part + 1) * half)
            ar = abar_ref[:, re]
            ai = abar_ref[:, im]
            xr = state_ref[:, re]
            xi = state_ref[:, im]
            for k in range(sub):
                kk = sub - 1 - k if reverse else k
                r = slice((s * sub + kk) * nb, (s * sub + kk + 1) * nb)
                xr, xi = (ar * xr - ai * xi + x_ref[r, re], ar * xi + ai * xr + x_ref[r, im])
                x_ref[r, re] = xr
                x_ref[r, im] = xi
            state_ref[:, re] = xr
            state_ref[:, im] = xi

    def read_out(s):
        y = jnp.dot(x_ref[rows(s), :].astype(BF16), cfull_ref[...],
                    preferred_element_type=F32)
        if final:
            y = jax.nn.gelu(y + yprev_ref[rows(s), :] + u_ref[rows(s), :] * dskip_ref[...])
            gate = jnp.dot(y.astype(BF16), gluw_ref[...], preferred_element_type=F32)
            y = y * jax.nn.sigmoid(gate + glub_ref[...])
        out_ref[rows(s), :] = y

    project_in(order[0])
    for idx, s in enumerate(order):
        if idx + 1 < len(order):
            project_in(order[idx + 1])
        if idx > 0:
            read_out(order[idx - 1])
        scan(s)
    read_out(order[-1])


def _s5_pass(u_scan, yprev, bfull, abar, cfull, dskip, gluw, glub, *,
             nb, ctx_len, steps, direction):
    rows, width = u_scan.shape
    nblk = rows // (steps * nb)
    nctx = ctx_len // steps
    reverse = direction == 1
    final = direction == 1
    row_map = lambda i: (_s5_block_index(i, nctx=nctx, nblk=nblk, reverse=reverse), 0)
    const = lambda i: (0, 0)
    blk = pl.BlockSpec((steps * nb, width), row_map)
    in_specs = [blk]
    args = [u_scan]
    if final:
        in_specs.append(blk)
        args.append(yprev)
    in_specs += [pl.BlockSpec(bfull.shape, const), pl.BlockSpec(abar.shape, const),
                 pl.BlockSpec(cfull.shape, const)]
    args += [bfull, abar, cfull]
    if final:
        in_specs += [pl.BlockSpec(dskip.shape, const), pl.BlockSpec(gluw.shape, const),
                     pl.BlockSpec(glub.shape, const)]
        args += [dskip, gluw, glub]
    kern = functools.partial(_s5_kernel, steps=steps, nb=nb, reverse=reverse, final=final)
    return pl.pallas_call(
        kern,
        grid=(nblk,),
        in_specs=in_specs,
        out_specs=blk,
        out_shape=jax.ShapeDtypeStruct((rows, width), F32),
        scratch_shapes=[pltpu.VMEM(abar.shape, F32),
                        pltpu.VMEM((steps * nb, abar.shape[1]), F32)],
        compiler_params=_cparams(("arbitrary",)),
        name="s5_bwd" if final else "s5_fwd",
    )(*args)


def _s5_params(lam_re, lam_im, log_step, b_re, b_im, c_re, c_im, nb):
    g, p = lam_re.shape[1], lam_re.shape[2]
    step = jnp.exp(log_step)[..., None]
    mag = jnp.exp(lam_re * step)
    ar = mag * jnp.cos(lam_im * step)
    ai = mag * jnp.sin(lam_im * step)
    den = lam_re * lam_re + lam_im * lam_im
    kr = ((ar - 1.0) * lam_re + ai * lam_im) / den
    ki = (ai * lam_re - (ar - 1.0) * lam_im) / den
    bbr = kr[..., None] * b_re - ki[..., None] * b_im
    bbi = kr[..., None] * b_im + ki[..., None] * b_re
    eye = jnp.eye(g, dtype=F32)
    cdim = b_re.shape[-1]

    def bmat(bb):
        return jnp.einsum('gpc,gh->gchp', bb, eye).reshape(g * cdim, g * p)

    bfull = jnp.stack([jnp.concatenate([bmat(bbr[d]), bmat(bbi[d])], axis=1)
                       for d in range(2)]).astype(BF16)
    abar = jnp.stack([jnp.broadcast_to(
        jnp.concatenate([ar[d].reshape(-1), ai[d].reshape(-1)])[None, :], (nb, 2 * g * p))
        for d in range(2)])

    def cmat(cc):
        return jnp.einsum('gcp,gh->gphc', cc, eye).reshape(g * p, g * cdim)

    cfull = jnp.concatenate([cmat(c_re), -cmat(c_im)], axis=0).astype(BF16)
    return bfull, abar, cfull


def _out_proj_kernel(x_ref, ssd_ref, s5_ref, mod_ref, w1_ref, w2_ref, g_ref, b_ref, o_ref, *,
                     d, alpha, n_ctx_tiles, lat_tiles, ctx_row):
    row = _mod_row(pl.program_id(0), n_ctx_tiles, lat_tiles, ctx_row)
    m2 = mod_ref[pl.ds(row, 1), 2 * d:3 * d]
    mix = (jnp.dot(ssd_ref[...].astype(BF16), w1_ref[...], preferred_element_type=F32)
           + jnp.dot(s5_ref[...].astype(BF16), w2_ref[...], preferred_element_type=F32))
    o_ref[...] = _layer_norm(alpha * x_ref[...] + m2 * mix, g_ref[...], b_ref[...])


def _out_proj(x, ssd, s5, mod, w1, w2, g, b, *, tm, alpha, n_ctx_tiles, lat_tiles, ctx_row):
    t, d = x.shape
    kern = functools.partial(_out_proj_kernel, d=d, alpha=alpha, n_ctx_tiles=n_ctx_tiles,
                             lat_tiles=lat_tiles, ctx_row=ctx_row)
    const = lambda i: (0, 0)
    return pl.pallas_call(
        kern,
        grid=(t // tm,),
        in_specs=[pl.BlockSpec((tm, d), lambda i: (i, 0)),
                  pl.BlockSpec((tm, ssd.shape[1]), lambda i: (i, 0)),
                  pl.BlockSpec((tm, s5.shape[1]), lambda i: (i, 0)),
                  pl.BlockSpec(mod.shape, const), pl.BlockSpec(w1.shape, const),
                  pl.BlockSpec(w2.shape, const), pl.BlockSpec(g.shape, const),
                  pl.BlockSpec(b.shape, const)],
        out_specs=pl.BlockSpec((tm, d), lambda i: (i, 0)),
        out_shape=jax.ShapeDtypeStruct((t, d), F32),
        compiler_params=_cparams(("arbitrary",)),
        name="out_proj_ln",
    )(x, ssd, s5, mod, w1, w2, g, b)


def _peer_kernel(x_ref, mod_ref, wqt_ref, keys_ref, u_ref, vt_ref, g_ref, b_ref, o_ref,
                 hbt_ref, acc_ref, q_ref, top_ref, cand_ref, ord_ref, work_ref, s0_ref, s1_ref,
                 rnk_ref, rank_ref, e1_ref, n_ref, r_ref, *,
                 d, nk, heads, half, alpha, n_ctx_tiles, lat_tiles, ctx_row):
    i = pl.program_id(0)
    j = pl.program_id(1)
    tm = x_ref.shape[0]
    eb = u_ref.shape[0]
    nsel = PEER_TOPK + 1
    pad_rows = top_ref.shape[1]
    pk = 16

    @pl.when(j == 0)
    def _prepare():
        row = _mod_row(i, n_ctx_tiles, lat_tiles, ctx_row)
        m3 = mod_ref[pl.ds(row, 1), 3 * d:4 * d]
        m4 = mod_ref[pl.ds(row, 1), 4 * d:5 * d]
        h = x_ref[...] * (1.0 + m4) + m3
        hbt = h.T.astype(BF16)
        hbt_ref[...] = hbt
        q_ref[...] = jnp.dot(wqt_ref[...], hbt, preferred_element_type=F32)
        acc_ref[...] = jnp.zeros_like(acc_ref)

        def per_head(hd, carry):
            top_ref[...] = jnp.full(top_ref.shape, NEG_INF, F32)
            rnk_ref[...] = jnp.full(rnk_ref.shape, 2.0 * nsel, F32)
            for c, dst in ((0, s0_ref), (1, s1_ref)):
                off = pl.multiple_of((hd * 2 + c) * nk, nk)
                qoff = pl.multiple_of((hd * 2 + c) * half, half)
                sc = jnp.dot(keys_ref[pl.ds(off, nk), :], q_ref[pl.ds(qoff, half), :],
                             precision=HIGHEST, preferred_element_type=F32)
                dst[...] = sc
                work_ref[c] = sc

            def take(r, carry):
                for c in range(2):
                    cur = work_ref[c]
                    m = jnp.max(cur, axis=0, keepdims=True)
                    top_ref[c, pl.ds(r, 1), :] = m
                    hit = cur >= m
                    work_ref[c] = jnp.where(hit, NEG_INF, cur)
                    if c == 1:
                        rnk_ref[...] = jnp.where(hit, lax.convert_element_type(r + 1, F32),
                                                 rnk_ref[...])
                return carry

            lax.fori_loop(0, nsel, take, 0, unroll=True)
            v0 = top_ref[0]
            v1 = top_ref[1]
            cands = [v0[0:1, :] + v1]
            cands += [v0[a:a + 1, :] + v1[0:8, :] for a in range(1, 8)]
            cands += [v0[8:pad_rows, :] + v1[0:1, :]]
            cand = jnp.concatenate(cands, axis=0)
            cand_ref[...] = cand

            def take2(r, carry):
                cur = cand_ref[...]
                m = jnp.max(cur, axis=0, keepdims=True)
                ord_ref[pl.ds(r, 1), :] = m
                cand_ref[...] = jnp.where(cur >= m, NEG_INF, cur)
                return carry

            lax.fori_loop(0, nsel, take2, 0, unroll=True)
            tau = 0.5 * (ord_ref[nsel - 2:nsel - 1, :] + ord_ref[nsel - 1:nsel, :])
            top = v0[0:1, :] + v1[0:1, :]
            z = jnp.sum(jnp.where(cand > tau, jnp.exp(cand - top), 0.0), axis=0, keepdims=True)
            s0 = s0_ref[...]
            theta = tau - s0
            count = jnp.zeros_like(theta)
            for bb in range(PEER_TOPK):
                count = count + (v1[bb:bb + 1, :] >= theta).astype(F32)
            n_ref[hd] = count
            r_ref[hd] = jnp.exp(s0 - v0[0:1, :]) / z
            e1 = jnp.exp(s1_ref[...] - v1[0:1, :])
            e1_ref[hd] = e1.astype(BF16).reshape(nk // pk, pk, tm)
            rank_ref[hd] = rnk_ref[...].astype(BF16).reshape(nk // pk, pk, tm)
            return carry

        lax.fori_loop(0, heads, per_head, 0)

    lanes = 128
    keys_per_step = eb // nk
    sub_keys = 4
    base = pl.multiple_of(j * keys_per_step, keys_per_step)
    hbt = hbt_ref[...]
    sub_rows = sub_keys * nk
    n_sub = keys_per_step // sub_keys

    def expert_acts(sub):
        return jnp.dot(u_ref[sub * sub_rows:(sub + 1) * sub_rows, :], hbt,
                       preferred_element_type=F32)

    def project(sub, gate):
        acc_ref[...] += jnp.dot(vt_ref[:, sub * sub_rows:(sub + 1) * sub_rows], gate,
                                preferred_element_type=F32)

    a_next = expert_acts(0)
    gate_prev = None
    for sub in range(n_sub):
        a_sub = a_next
        if sub + 1 < n_sub:
            a_next = expert_acts(sub + 1)
        if gate_prev is not None:
            project(sub - 1, gate_prev)
        rows = []
        for k2 in range(sub_keys):
            ii = sub * sub_keys + k2
            cols = []
            for lt in range(tm // lanes):
                ls = slice(lt * lanes, (lt + 1) * lanes)
                w = jnp.zeros((nk // pk, pk, lanes), BF16)
                for hd in range(heads):
                    cnt = n_ref[hd, pl.ds(base, keys_per_step), ls][ii:ii + 1, :]
                    rr = r_ref[hd, pl.ds(base, keys_per_step), ls][ii:ii + 1, :]
                    cnt = jnp.broadcast_to(cnt, (pk, lanes)).astype(BF16)[None]
                    rr = jnp.broadcast_to(rr, (pk, lanes)).astype(BF16)[None]
                    w = w + jnp.where(rank_ref[hd, :, :, ls] <= cnt, e1_ref[hd, :, :, ls],
                                      jnp.zeros((), BF16)) * rr
                a = a_sub[k2 * nk:(k2 + 1) * nk, ls].astype(BF16).reshape(nk // pk, pk, lanes)
                cols.append((w * jax.nn.gelu(a)).reshape(nk, lanes))
            rows.append(jnp.concatenate(cols, axis=1))
        gate_prev = jnp.concatenate(rows, axis=0)
    project(n_sub - 1, gate_prev)

    @pl.when(j == pl.num_programs(1) - 1)
    def _finish():
        row = _mod_row(i, n_ctx_tiles, lat_tiles, ctx_row)
        m5 = mod_ref[pl.ds(row, 1), 5 * d:6 * d]
        f = acc_ref[...].T
        o_ref[...] = _layer_norm(alpha * x_ref[...] + m5 * f, g_ref[...], b_ref[...])


def _peer(x, mod, wqt, keys2d, u_b, vt_b, g, b, *, layer, tm, heads, alpha,
          skip_tiles, n_ctx_tiles, lat_tiles, ctx_row):
    t, d = x.shape
    _, nblk, _, eb = vt_b.shape
    nk = keys2d.shape[0] // (2 * heads)
    half = keys2d.shape[1]
    pad_rows = 24
    ncand = pad_rows + 7 * 8 + (pad_rows - 8)
    kern = functools.partial(_peer_kernel, d=d, nk=nk, heads=heads, half=half, alpha=alpha,
                             n_ctx_tiles=n_ctx_tiles, lat_tiles=lat_tiles, ctx_row=ctx_row)
    const = lambda i, j: (0, 0)
    return pl.pallas_call(
        kern,
        grid=(t // tm - skip_tiles, nblk),
        in_specs=[pl.BlockSpec((tm, d), lambda i, j: (i + skip_tiles, 0)),
                  pl.BlockSpec(mod.shape, const), pl.BlockSpec(wqt.shape, const),
                  pl.BlockSpec(keys2d.shape, const),
                  pl.BlockSpec((None, eb, d), lambda i, j: (layer, j, 0)),
                  pl.BlockSpec((None, None, d, eb), lambda i, j: (layer, j, 0, 0)),
                  pl.BlockSpec(g.shape, const), pl.BlockSpec(b.shape, const)],
        out_specs=pl.BlockSpec((tm, d), lambda i, j: (i, 0)),
        out_shape=jax.ShapeDtypeStruct((t - skip_tiles * tm, d), F32),
        scratch_shapes=[pltpu.VMEM((d, tm), BF16),
                        pltpu.VMEM((d, tm), F32),
                        pltpu.VMEM((wqt.shape[0], tm), F32),
                        pltpu.VMEM((2, pad_rows, tm), F32),
                        pltpu.VMEM((ncand, tm), F32),
                        pltpu.VMEM((pad_rows, tm), F32),
                        pltpu.VMEM((2, nk, tm), F32),
                        pltpu.VMEM((nk, tm), F32),
                        pltpu.VMEM((nk, tm), F32),
                        pltpu.VMEM((nk, tm), F32),
                        pltpu.VMEM((heads, nk // 16, 16, tm), BF16),
                        pltpu.VMEM((heads, nk // 16, 16, tm), BF16),
                        pltpu.VMEM((heads, nk, tm), F32),
                        pltpu.VMEM((heads, nk, tm), F32)],
        compiler_params=_cparams(("arbitrary", "arbitrary")),
        name="peer_ln",
    )(x, mod, wqt, keys2d, u_b, vt_b, g, b)


def _pick_tile(limit, *sizes):
    tm = limit
    while any(s % tm for s in sizes):
        tm //= 2
    return tm


def kernel(x, c, ctx, c_ctx, w_mod, b_mod, w_in, conv_w, conv_b, ssd_a_log, ssd_dt_bias, ssd_d, ssd_norm_g, s5_lam_re, s5_lam_im, s5_log_step, s5_b_re, s5_b_im, s5_c_re, s5_c_im, s5_d, s5_glu_w, s5_glu_b, w_out, ln1_g, ln1_b, peer_wq, peer_keys, peer_u, peer_v, ln2_g, ln2_b):
    nb, seq, d = x.shape
    ctx_len = ctx.shape[1]
    depth = w_mod.shape[0]
    s5_width = s5_d.shape[1]
    heads = peer_keys.shape[1]
    nk = peer_keys.shape[3]
    alpha = (2.0 * depth) ** 0.25
    rows_lat = seq // GRID_W

    tm = _pick_tile(512, nb * ctx_len, seq)
    tp = _pick_tile(512, nb * ctx_len, seq)
    q = _pick_tile(128, ctx_len, seq)
    ssd_sub = 2 if ctx_len % (2 * q) == 0 and seq % (2 * q) == 0 else 1
    steps = _pick_tile(256, ctx_len, seq)
    eb = 16 * nk
    ctx_row = nb

    mod_rows = -(-(nb + 1) // 8) * 8
    cc = jnp.concatenate([c, c_ctx[None, :], jnp.zeros((mod_rows - nb - 1, d), F32)], axis=0)
    mods = _mod_table(cc, w_mod, b_mod)

    xt = jnp.concatenate([ctx.reshape(nb * ctx_len, d), x.reshape(nb * seq, d)], axis=0)

    def pad_lanes(w, n):
        return jnp.pad(w, ((0, 0), (0, n - w.shape[1])))

    u_all = peer_u.astype(BF16)
    vt_all = peer_v.astype(BF16).reshape(depth, nk * nk // eb, eb, d).transpose(0, 1, 3, 2)

    for i in range(depth):
        wi = w_in[i]
        o_dt = SSD_WIDTH + XBC_WIDTH
        w_pad = jnp.concatenate(
            [wi[:, :o_dt],
             pad_lanes(wi[:, o_dt:o_dt + SSD_HEADS], HEAD_LANES),
             pad_lanes(wi[:, o_dt + SSD_HEADS:o_dt + 2 * SSD_HEADS], HEAD_LANES),
             wi[:, o_dt + 2 * SSD_HEADS:]], axis=1).astype(BF16)
        a_rows = pad_lanes(-jnp.exp(ssd_a_log[i]), HEAD_LANES)
        dtb_rows = pad_lanes(ssd_dt_bias[i], HEAD_LANES)
        dskip_ssd = jnp.repeat(ssd_d[i], SSD_HEAD_DIM)[None, :]
        bfull, abar, cfull = _s5_params(s5_lam_re[i], s5_lam_im[i], s5_log_step[i],
                                        s5_b_re[i], s5_b_im[i], s5_c_re[i], s5_c_im[i], nb)

        z, xbc, dt, u = _in_proj(xt, mods[i], w_pad, tm=tm, n_ctx_tiles=nb * ctx_len // tm,
                                 lat_tiles=seq // tm, ctx_row=ctx_row, s5_width=s5_width)
        ssd_args = dict(nb=nb, ctx_len=ctx_len, seq=seq, q=q, n_sub=ssd_sub)
        y_f = _ssd_pass(xbc, dt, None, None, conv_w[i], conv_b[i][None, :], a_rows[0:1],
                        dtb_rows[0:1], None, None, direction=0, **ssd_args)
        ssd_out = _ssd_pass(xbc, dt, z, y_f, conv_w[i], conv_b[i][None, :], a_rows[1:2],
                            dtb_rows[1:2], dskip_ssd, ssd_norm_g[i][None, :], direction=1,
                            **ssd_args)

        u_ctx = u[:nb * ctx_len].reshape(nb, ctx_len, s5_width).transpose(1, 0, 2)
        u_lat = u[nb * ctx_len:].reshape(nb, rows_lat, GRID_W, s5_width).transpose(2, 1, 0, 3)
        u_scan = jnp.concatenate([u_ctx.reshape(ctx_len * nb, s5_width),
                                  u_lat.reshape(seq * nb, s5_width)], axis=0)
        s5_args = dict(nb=nb, ctx_len=ctx_len, steps=steps)
        ys_f = _s5_pass(u_scan, None, bfull[0], abar[0], cfull, None, None, None,
                        direction=0, **s5_args)
        s5_scan = _s5_pass(u_scan, ys_f, bfull[1], abar[1], cfull, s5_d[i][None, :],
                           s5_glu_w[i].astype(BF16), s5_glu_b[i][None, :], direction=1,
                           **s5_args)
        s5_ctx = s5_scan[:ctx_len * nb].reshape(ctx_len, nb, s5_width).transpose(1, 0, 2)
        s5_lat = s5_scan[ctx_len * nb:].reshape(GRID_W, rows_lat, nb, s5_width)
        s5_lat = s5_lat.transpose(2, 1, 0, 3)
        s5_out = jnp.concatenate([s5_ctx.reshape(nb * ctx_len, s5_width),
                                  s5_lat.reshape(nb * seq, s5_width)], axis=0)

        wo = w_out[i].astype(BF16)
        xt = _out_proj(xt, ssd_out, s5_out, mods[i], wo[:SSD_WIDTH], wo[SSD_WIDTH:],
                       ln1_g[i][None, :], ln1_b[i][None, :], tm=tm, alpha=alpha,
                       n_ctx_tiles=nb * ctx_len // tm, lat_tiles=seq // tm, ctx_row=ctx_row)

        last = i == depth - 1
        ctx_tiles = nb * ctx_len // tp
        xt = _peer(xt, mods[i], peer_wq[i].T.astype(BF16),
                   peer_keys[i].reshape(heads * 2 * nk, -1), u_all, vt_all,
                   ln2_g[i][None, :], ln2_b[i][None, :], layer=i, tm=tp, heads=heads, alpha=alpha,
                   skip_tiles=ctx_tiles if last else 0,
                   n_ctx_tiles=0 if last else ctx_tiles, lat_tiles=seq // tp, ctx_row=ctx_row)

    return xt.reshape(nb, seq, d)
```
